```python
import jax, jax.numpy as jnp
from jax import lax
import numpy as np

D_MODEL = 2048
BATCH = 4
SEQ = 2048
DEPTH = 2
DEC_BATCH = 32
DEC_SEQ = 4
PAST_LEN = 16384
PAGE_SIZE = 128

SWA_HEADS = 16
SWA_KV_HEADS = 4
SWA_HEAD_DIM = 64
WINDOW = 128
ROPE_THETA = 500000.0
ROPE_DIM = SWA_HEAD_DIM // 4
RET_HEADS = 8
RET_HEAD_DIM = 128
RET_CHUNK = 128
RET_ROT_BASE = 10000.0
SWA_Q = SWA_HEADS * SWA_HEAD_DIM
SWA_KV = SWA_KV_HEADS * SWA_HEAD_DIM
RET_W = RET_HEADS * RET_HEAD_DIM
N_BRANCH = 2
IN_SIZES = (SWA_Q, SWA_KV, SWA_KV, RET_W, RET_W, RET_W, RET_W, N_BRANCH * D_MODEL)
IN_COLS = SWA_Q + 2 * SWA_KV + 4 * RET_W + N_BRANCH * D_MODEL
N_GROUPS = 4
EXPERTS_PER_GROUP = 4
N_EXPERTS = N_GROUPS * EXPERTS_PER_GROUP
EXPERT_TOPK = 2
EXPERT_FF = 512
EPS = 1e-6

kernel_name = 'hybrid_swa_retention_hmoe_decode_step'

F32 = jnp.float32


def rms_norm(x, gain=None):
    xf = x.astype(F32)
    y = xf * lax.rsqrt(jnp.mean(xf * xf, axis=-1, keepdims=True) + EPS)
    if gain is not None:
        y = y * gain.astype(F32)
    return y.astype(x.dtype)


def rotary(x, pos, rot_dim, base):
    half = rot_dim // 2
    inv = 1.0 / (base ** (jnp.arange(half, dtype=F32) * (2.0 / rot_dim)))
    ang = pos.astype(F32)[:, None] * inv[None, :]
    cos = jnp.cos(ang)[None, :, None, :]
    sin = jnp.sin(ang)[None, :, None, :]
    xr = x[..., :rot_dim].astype(F32)
    x1, x2 = xr[..., :half], xr[..., half:]
    rot = jnp.concatenate([x1 * cos - x2 * sin, x2 * cos + x1 * sin], axis=-1).astype(x.dtype)
    return jnp.concatenate([rot, x[..., rot_dim:]], axis=-1)


def split_cols(a):
    outs, start = [], 0
    for s in IN_SIZES:
        outs.append(a[..., start:start + s])
        start += s
    return outs


def sink_attention(q, k, v, mask, sinks):
    hd = q.shape[-1]
    s = jnp.einsum('bnqhgd,bnshd->bnhgqs', q, k, preferred_element_type=F32) * (hd ** -0.5)
    s = jnp.where(mask[None, :, None, None], s, -jnp.inf)
    sk = sinks.astype(F32)[None, None, :, :, None, None]
    m = jnp.maximum(jnp.max(s, axis=-1, keepdims=True), sk)
    p = jnp.exp(s - m)
    den = jnp.sum(p, axis=-1, keepdims=True) + jnp.exp(sk - m)
    return jnp.einsum('bnhgqs,bnshd->bnqhgd', (p / den).astype(v.dtype), v)


def swa_prompt(q, k, v, sinks):
    B, T, Hq, hd = q.shape
    Hkv = k.shape[2]
    G = Hq // Hkv
    nb = T // WINDOW
    qb = q.reshape(B, nb, WINDOW, Hkv, G, hd)
    kb = k.reshape(B, nb, WINDOW, Hkv, hd)
    vb = v.reshape(B, nb, WINDOW, Hkv, hd)
    prev = lambda a: jnp.concatenate([jnp.zeros_like(a[:, :1]), a[:, :-1]], axis=1)
    kk = jnp.concatenate([prev(kb), kb], axis=2)
    vv = jnp.concatenate([prev(vb), vb], axis=2)
    qi = jnp.arange(WINDOW)[:, None] + WINDOW
    sj = jnp.arange(2 * WINDOW)[None, :]
    band = (sj <= qi) & (sj > qi - WINDOW)
    blk = jnp.arange(nb)[:, None, None]
    mask = band[None] & ((blk > 0) | (sj[None] >= WINDOW))
    o = sink_attention(qb, kk, vv, mask, sinks.reshape(Hkv, G))
    return o.reshape(B, T, Hq * hd)


def swa_sample(q, k, v, kbuf, vbuf, sinks):
    B, T, Hq, hd = q.shape
    Hkv = k.shape[2]
    G = Hq // Hkv
    Wb = kbuf.shape[1]
    kk = jnp.concatenate([kbuf.astype(k.dtype), k], axis=1)
    vv = jnp.concatenate([vbuf.astype(v.dtype), v], axis=1)
    qpos = jnp.arange(T)[:, None]
    kpos = jnp.arange(Wb + T)[None, :] - Wb
    mask = ((kpos <= qpos) & (kpos > qpos - WINDOW))[None]
    o = sink_attention(q.reshape(B, 1, T, Hkv, G, hd), kk[:, None], vv[:, None], mask,
                       sinks.reshape(Hkv, G))
    return o.reshape(B, T, Hq * hd), kk[:, -Wb:], vv[:, -Wb:]


def retention(q, k, v, s0, chunk):
    B, T, H, dk = q.shape
    dv = v.shape[-1]
    nc = T // chunk
    log_g = jnp.log(1.0 - jnp.exp2(-5.0 - jnp.arange(H, dtype=F32)))
    i = jnp.arange(chunk, dtype=F32)
    diff = i[:, None] - i[None, :]
    intra = jnp.where(diff >= 0, jnp.exp(jnp.maximum(diff, 0.0) * log_g[:, None, None]), 0.0)
    q_dec = jnp.exp((i + 1.0) * log_g[:, None])[None, :, :, None]
    k_dec = jnp.exp((chunk - 1.0 - i) * log_g[:, None])[None, :, :, None]
    c_dec = jnp.exp(chunk * log_g)[None, :, None, None]

    def blocks(a):
        return a.astype(F32).reshape(B, nc, chunk, H, a.shape[-1]).transpose(1, 0, 3, 2, 4)

    def step(S, inp):
        qc, kc, vc = inp
        att = jnp.einsum('bhid,bhjd->bhij', qc, kc) * intra
        o = jnp.einsum('bhij,bhje->bhie', att, vc) + jnp.einsum('bhid,bhde->bhie', qc * q_dec, S)
        S = S * c_dec + jnp.einsum('bhjd,bhje->bhde', kc * k_dec, vc)
        return S, o

    S, o = lax.scan(step, s0.astype(F32), (blocks(q), blocks(k), blocks(v)))
    o = o.transpose(1, 0, 3, 2, 4).reshape(B, T, H, dv)
    return o, S


def token_mix(h, pos, w_in, sinks, w_br_a, w_br_b, w_out, cache):
    B, T, D = h.shape
    q, k, v, rq, rk, rv, rg, gl = split_cols(h @ w_in)
    q = rotary(q.reshape(B, T, SWA_HEADS, SWA_HEAD_DIM), pos, ROPE_DIM, ROPE_THETA)
    k = rotary(k.reshape(B, T, SWA_KV_HEADS, SWA_HEAD_DIM), pos, ROPE_DIM, ROPE_THETA)
    v = v.reshape(B, T, SWA_KV_HEADS, SWA_HEAD_DIM)
    if cache is None:
        a = swa_prompt(q, k, v, sinks)
        wp = min(WINDOW, T)
        nk, nv = k[:, -wp:], v[:, -wp:]
        s0 = jnp.zeros((B, RET_HEADS, RET_HEAD_DIM, RET_HEAD_DIM), F32)
    else:
        kbuf, vbuf, s0 = cache
        a, nk, nv = swa_sample(q, k, v, kbuf, vbuf, sinks)
    rq = rotary(rq.reshape(B, T, RET_HEADS, RET_HEAD_DIM), pos, RET_HEAD_DIM, RET_ROT_BASE)
    rk = rotary(rk.reshape(B, T, RET_HEADS, RET_HEAD_DIM), pos, RET_HEAD_DIM, RET_ROT_BASE) * (RET_HEAD_DIM ** -0.5)
    rv = rv.reshape(B, T, RET_HEADS, RET_HEAD_DIM)
    chunk = RET_CHUNK if T % RET_CHUNK == 0 else T
    r, s_new = retention(rq, rk, rv, s0, chunk)
    r = rms_norm(r).reshape(B, T, RET_W).astype(h.dtype) * jax.nn.silu(rg)
    gates = jax.nn.sigmoid(gl.astype(F32)).reshape(B, T, N_BRANCH, D).astype(h.dtype)
    merged = gates[:, :, 0] * (a @ w_br_a) + gates[:, :, 1] * (r @ w_br_b)
    return merged @ w_out, nk, nv, s_new.astype(h.dtype)


def hier_moe(h, w_rg, w_re, w_g, w_u, w_d):
    shp = h.shape
    hf = h.reshape(-1, shp[-1])
    N = hf.shape[0]
    g_logit = jnp.dot(hf, w_rg, preferred_element_type=F32)
    g_prob = jax.nn.softmax(g_logit, axis=-1)
    g_sel = jnp.argmax(g_logit, axis=-1)
    g_w = jnp.take_along_axis(g_prob, g_sel[:, None], axis=1)
    e_logit = jnp.dot(hf, w_re, preferred_element_type=F32).reshape(N, N_GROUPS, EXPERTS_PER_GROUP)
    e_logit = jnp.take_along_axis(e_logit, g_sel[:, None, None], axis=1)[:, 0]
    e_val, e_idx = lax.top_k(e_logit, EXPERT_TOPK)
    e_w = jax.nn.softmax(e_val, axis=-1) * g_w
    e_id = g_sel[:, None] * EXPERTS_PER_GROUP + e_idx
    combine = jnp.sum(jax.nn.one_hot(e_id, N_EXPERTS, dtype=F32) * e_w[..., None], axis=1)
    hid = jax.nn.silu(jnp.einsum('nd,edf->nef', hf, w_g)) * jnp.einsum('nd,edf->nef', hf, w_u)
    hid = hid * combine[:, :, None].astype(hid.dtype)
    y = jnp.einsum('nef,efd->nd', hid, w_d)
    return y.reshape(shp)


def trunk(x, c, pos, win_k, win_v, ret_s, params):
    (w_ada, b_ada, g_mix, g_ffn, w_in, attn_sinks, w_branch_attn, w_branch_ret, w_out,
     w_route_group, w_route_expert, w_exp_gate, w_exp_up, w_exp_down, g_final) = params
    new_k, new_v, new_s = [], [], []
    for l in range(DEPTH):
        mod = (jax.nn.silu(c) @ w_ada[l] + b_ada[l])[:, None, :]
        sh1, sc1, gt1, sh2, sc2, gt2 = jnp.split(mod, 6, axis=-1)
        h = rms_norm(x, g_mix[l]) * (1 + sc1) + sh1
        cache = None if win_k is None else (win_k[l], win_v[l], ret_s[l])
        m, nk, nv, ns = token_mix(h, pos, w_in[l], attn_sinks[l], w_branch_attn[l],
                                  w_branch_ret[l], w_out[l], cache)
        x = x + gt1 * m
        h = rms_norm(x, g_ffn[l]) * (1 + sc2) + sh2
        x = x + gt2 * hier_moe(h, w_route_group[l], w_route_expert[l], w_exp_gate[l],
                               w_exp_up[l], w_exp_down[l])
        new_k.append(nk)
        new_v.append(nv)
        new_s.append(ns)
    return rms_norm(x, g_final), jnp.stack(new_k), jnp.stack(new_v), jnp.stack(new_s)


def setup_inputs(seed: int = 0) -> dict:
    key = jax.random.key(seed)
    ks = jax.random.split(key, 24)
    D = D_MODEL
    win_buf = min(WINDOW, PAST_LEN)
    nrm = lambda k, shape, scale: jax.random.normal(k, shape, F32) * scale
    return {
        'x_prompt': nrm(ks[0], (BATCH, SEQ, D), 1.0),
        'x_sample': nrm(ks[1], (DEC_BATCH, DEC_SEQ, D), 1.0),
        'cache_win_k': nrm(ks[2], (DEPTH, DEC_BATCH, win_buf, SWA_KV_HEADS, SWA_HEAD_DIM), 1.0),
        'cache_win_v': nrm(ks[3], (DEPTH, DEC_BATCH, win_buf, SWA_KV_HEADS, SWA_HEAD_DIM), 1.0),
        'state_ret': nrm(ks[4], (DEPTH, DEC_BATCH, RET_HEADS, RET_HEAD_DIM, RET_HEAD_DIM), 0.3),
        'c_prompt': nrm(ks[5], (BATCH, D), 1.0),
        'c_sample': nrm(ks[6], (DEC_BATCH, D), 1.0),
        'w_ada': nrm(ks[7], (DEPTH, D, 6 * D), 0.5 * D ** -0.5),
        'b_ada': nrm(ks[8], (DEPTH, 6 * D), 0.02),
        'g_mix': 1.0 + nrm(ks[9], (DEPTH, D), 0.02),
        'g_ffn': 1.0 + nrm(ks[10], (DEPTH, D), 0.02),
        'w_in': nrm(ks[11], (DEPTH, D, IN_COLS), D ** -0.5),
        'attn_sinks': nrm(ks[12], (DEPTH, SWA_HEADS), 1.0),
        'w_branch_attn': nrm(ks[13], (DEPTH, SWA_Q, D), SWA_Q ** -0.5),
        'w_branch_ret': nrm(ks[14], (DEPTH, RET_W, D), RET_W ** -0.5),
        'w_out': nrm(ks[15], (DEPTH, D, D), D ** -0.5),
        'w_route_group': nrm(ks[16], (DEPTH, D, N_GROUPS), D ** -0.5),
        'w_route_expert': nrm(ks[17], (DEPTH, D, N_EXPERTS), D ** -0.5),
        'w_exp_gate': nrm(ks[18], (DEPTH, N_EXPERTS, D, EXPERT_FF), D ** -0.5),
        'w_exp_up': nrm(ks[19], (DEPTH, N_EXPERTS, D, EXPERT_FF), D ** -0.5),
        'w_exp_down': nrm(ks[20], (DEPTH, N_EXPERTS, EXPERT_FF, D), EXPERT_FF ** -0.5),
        'g_final': 1.0 + nrm(ks[21], (D,), 0.02),
    }


def reference(x_prompt, x_sample, cache_win_k, cache_win_v, state_ret, c_prompt, c_sample,
              w_ada, b_ada, g_mix, g_ffn, w_in, attn_sinks, w_branch_attn, w_branch_ret, w_out,
              w_route_group, w_route_expert, w_exp_gate, w_exp_up, w_exp_down, g_final):
    params = (w_ada, b_ada, g_mix, g_ffn, w_in, attn_sinks, w_branch_attn, w_branch_ret, w_out,
              w_route_group, w_route_expert, w_exp_gate, w_exp_up, w_exp_down, g_final)
    pos_p = jnp.arange(x_prompt.shape[1], dtype=jnp.int32)
    pos_s = PAST_LEN + jnp.arange(x_sample.shape[1], dtype=jnp.int32)
    y_prompt, pk, pv, ps = trunk(x_prompt, c_prompt, pos_p, None, None, None, params)
    y_sample, sk, sv, ss = trunk(x_sample, c_sample, pos_s, cache_win_k, cache_win_v, state_ret, params)
    return (y_prompt, y_sample, pk, pv, ps, sk, sv, ss)
```

```python
import functools

import jax
import jax.numpy as jnp
from jax import lax
from jax.experimental import pallas as pl
from jax.experimental.pallas import tpu as pltpu

F32 = jnp.float32
BF16 = jnp.bfloat16
I32 = jnp.int32

PAST_LEN = 16384
WINDOW = 128
ROPE_THETA = 500000.0
RET_CHUNK = 128
RET_ROT_BASE = 10000.0
EPS = 1e-6

LANES = 128
BF16_SUBLANES = 16
VMEM_LIMIT_BYTES = 56 * 1024 * 1024
ELEM_TILE_ROWS = 512
MATMUL_MAX_ROWS = 2048
SAMPLE_SEQS_PER_STEP = 8
MOE_TILE_ROWS = 256
ROUTER_LANES = 128
DMA_ISSUE_UNROLL = 8


def _cparams(n_axes):
    return pltpu.CompilerParams(dimension_semantics=("arbitrary",) * n_axes,
                                vmem_limit_bytes=VMEM_LIMIT_BYTES)


def _matmul_rows(n):
    for d in range(min(n, MATMUL_MAX_ROWS), 0, -1):
        if n % d == 0 and d % BF16_SUBLANES == 0:
            return d
    raise ValueError(f"no row tile for {n} rows")


def _silu(x):
    return x * jax.nn.sigmoid(x)


def _ada_kernel(c_ref, w_ref, b_ref, o_ref):
    s = _silu(c_ref[...]).astype(BF16)
    o_ref[0] = jnp.dot(s, w_ref[0].astype(BF16), preferred_element_type=F32) + b_ref[0]


def _ada(c_all, w_ada, b_ada):
    depth, d, n6 = w_ada.shape
    r = c_all.shape[0]
    tn = min(1024, d)
    assert n6 % tn == 0
    return pl.pallas_call(
        _ada_kernel,
        out_shape=jax.ShapeDtypeStruct((depth, r, n6), F32),
        grid=(depth, n6 // tn),
        in_specs=[pl.BlockSpec((r, d), lambda l, j: (0, 0)),
                  pl.BlockSpec((1, d, tn), lambda l, j: (l, 0, j)),
                  pl.BlockSpec((1, 1, tn), lambda l, j: (l, 0, j))],
        out_specs=pl.BlockSpec((1, r, tn), lambda l, j: (l, 0, j)),
        compiler_params=_cparams(2),
    )(c_all, w_ada, b_ada.reshape(depth, 1, n6))


def _route(logits, n_groups, per_group):
    lane = lax.broadcasted_iota(I32, logits.shape, 1).astype(F32)
    big = float(ROUTER_LANES)
    ninf = -jnp.inf
    gl = jnp.where(lane < n_groups, logits, ninf)
    gmax = jnp.max(gl, axis=1, keepdims=True)
    gsel = jnp.min(jnp.where(gl == gmax, lane, big), axis=1, keepdims=True)
    gden = jnp.sum(jnp.exp(gl - gmax), axis=1, keepdims=True)
    gw = 1.0 / gden
    lo = n_groups + gsel * per_group
    el = jnp.where(jnp.logical_and(lane >= lo, lane < lo + per_group), logits, ninf)
    v1 = jnp.max(el, axis=1, keepdims=True)
    i1 = jnp.min(jnp.where(el == v1, lane, big), axis=1, keepdims=True)
    el2 = jnp.where(lane == i1, ninf, el)
    v2 = jnp.max(el2, axis=1, keepdims=True)
    i2 = jnp.min(jnp.where(el2 == v2, lane, big), axis=1, keepdims=True)
    t = jnp.exp(v2 - v1)
    w1 = gw / (1.0 + t)
    w2 = gw * t / (1.0 + t)
    out = jnp.where(lane == 0, i1 - n_groups,
                    jnp.where(lane == 1, i2 - n_groups,
                              jnp.where(lane == 2, w1, jnp.where(lane == 3, w2, 0.0))))
    return out


def _norm_kernel(*refs, n_add, has_mod, emit_x, h_dtype, route_dims, final, n_ptiles, n_srows):
    it = iter(refs)
    xp_ref, xs_ref = next(it), next(it)
    add_refs = [next(it) for _ in range(n_add)]
    if n_add:
        gp_ref, gs_ref = next(it), next(it)
    gain_ref = next(it)
    if has_mod:
        scp_ref, scs_ref, shp_ref, shs_ref = next(it), next(it), next(it), next(it)
    if route_dims:
        wr_ref = next(it)
    if emit_x:
        xpo_ref, xso_ref = next(it), next(it)
    if h_dtype is not None:
        h_ref = next(it)
    if route_dims:
        r_ref = next(it)
    if final:
        yp_ref, ys_ref = next(it), next(it)

    def compute(x, adds, gate, sc, sh):
        if adds:
            s = adds[0]
            for a in adds[1:]:
                s = s + a
            x = x + gate * s
        h = x * lax.rsqrt(jnp.mean(x * x, axis=-1, keepdims=True) + EPS) * gain_ref[...]
        if has_mod:
            h = h * (1.0 + sc) + sh
        return x, h

    def logits_of(h):
        return jnp.dot(h, wr_ref[...], preferred_element_type=F32,
                       precision=lax.Precision.HIGHEST)

    i = pl.program_id(0)

    @pl.when(i < n_ptiles)
    def _():
        adds = [a[...].astype(F32) for a in add_refs]
        x, h = compute(xp_ref[...], adds,
                       gp_ref[...] if n_add else None,
                       scp_ref[...] if has_mod else None,
                       shp_ref[...] if has_mod else None)
        if emit_x:
            xpo_ref[...] = x
        if h_dtype is not None:
            h_ref[...] = h.astype(h_dtype)
        if route_dims:
            r_ref[...] = _route(logits_of(h), *route_dims)
        if final:
            yp_ref[...] = h

    @pl.when(i == n_ptiles)
    def _():
        adds = [a[0:n_srows].astype(F32) for a in add_refs]
        x, h = compute(xs_ref[...], adds,
                       gs_ref[...] if n_add else None,
                       scs_ref[...] if has_mod else None,
                       shs_ref[...] if has_mod else None)
        if emit_x:
            xso_ref[...] = x
        if h_dtype is not None:
            h_ref[0:n_srows] = h.astype(h_dtype)
        if route_dims:
            r_ref[0:n_srows] = _route(logits_of(h), *route_dims)
        if final:
            ys_ref[...] = h


def _norm_call(xp, xs, seq_len, adds, gate, gain, mod, *, emit_x, h_dtype=None,
               route=None, final=False):
    n_p, d = xp.shape
    n_s = xs.shape[0]
    te = min(ELEM_TILE_ROWS, seq_len)
    assert seq_len % te == 0 and n_s <= te and n_s % 8 == 0
    tiles_per_seq = seq_len // te
    n_ptiles = n_p // te
    n_tot = n_p + n_s

    def pidx(i):
        return jnp.minimum(i, n_ptiles - 1)

    def modp_spec(layer, chunk):
        return pl.BlockSpec((None, None, 1, d),
                            lambda i: (layer, pidx(i) // tiles_per_seq, 0, chunk))

    def mods_spec(layer, chunk):
        return pl.BlockSpec((None, n_s, d), lambda i: (layer, 0, chunk))

    args = [xp, xs]
    specs = [pl.BlockSpec((te, d), lambda i: (pidx(i), 0)),
             pl.BlockSpec((n_s, d), lambda i: (0, 0))]
    for arr, plane in adds:
        args.append(arr)
        if plane is None:
            specs.append(pl.BlockSpec((te, d), lambda i: (i, 0)))
        else:
            specs.append(pl.BlockSpec((None, te, d), lambda i, plane=plane: (plane, i, 0)))
    if adds:
        mp, ms, layer, chunk = gate
        args += [mp, ms]
        specs += [modp_spec(layer, chunk), mods_spec(layer, chunk)]
    args.append(gain)
    specs.append(pl.BlockSpec((1, d), lambda i: (0, 0)))
    if mod is not None:
        mp, ms, layer, sc_chunk, sh_chunk = mod
        args += [mp, ms, mp, ms]
        specs += [modp_spec(layer, sc_chunk), mods_spec(layer, sc_chunk),
                  modp_spec(layer, sh_chunk), mods_spec(layer, sh_chunk)]
    route_dims = None
    if route is not None:
        wr, n_groups, per_group = route
        route_dims = (n_groups, per_group)
        args.append(wr)
        specs.append(pl.BlockSpec(wr.shape, lambda i: (0, 0)))

    out_shapes, out_specs = [], []
    if emit_x:
        out_shapes += [jax.ShapeDtypeStruct((n_p, d), F32), jax.ShapeDtypeStruct((n_s, d), F32)]
        out_specs += [pl.BlockSpec((te, d), lambda i: (pidx(i), 0)),
                      pl.BlockSpec((n_s, d), lambda i: (0, 0))]
    if h_dtype is not None:
        out_shapes.append(jax.ShapeDtypeStruct((n_tot, d), h_dtype))
        out_specs.append(pl.BlockSpec((te, d), lambda i: (i, 0)))
    if route is not None:
        out_shapes.append(jax.ShapeDtypeStruct((n_tot, ROUTER_LANES), F32))
        out_specs.append(pl.BlockSpec((te, ROUTER_LANES), lambda i: (i, 0)))
    if final:
        out_shapes += [jax.ShapeDtypeStruct((n_p, d), F32), jax.ShapeDtypeStruct((n_s, d), F32)]
        out_specs += [pl.BlockSpec((te, d), lambda i: (pidx(i), 0)),
                      pl.BlockSpec((n_s, d), lambda i: (0, 0))]

    kern = functools.partial(_norm_kernel, n_add=len(adds), has_mod=mod is not None,
                             emit_x=emit_x, h_dtype=h_dtype, route_dims=route_dims,
                             final=final, n_ptiles=n_ptiles, n_srows=n_s)
    return pl.pallas_call(
        kern, out_shape=out_shapes, grid=(n_ptiles + 1,),
        in_specs=specs, out_specs=out_specs, compiler_params=_cparams(1),
    )(*args)


def _mm_kernel(x_ref, w_ref, o_ref):
    o_ref[...] = jnp.dot(x_ref[...], w_ref[...].astype(BF16),
                         preferred_element_type=F32).astype(o_ref.dtype)


def _matmul(x, w, layer, tn, out_dtype=BF16):
    m, k = x.shape
    n = w.shape[2]
    tm = _matmul_rows(m)
    assert n % tn == 0
    return pl.pallas_call(
        _mm_kernel,
        out_shape=jax.ShapeDtypeStruct((m, n), out_dtype),
        grid=(m // tm, n // tn),
        in_specs=[pl.BlockSpec((tm, k), lambda i, j: (i, 0)),
                  pl.BlockSpec((None, k, tn), lambda i, j: (layer, 0, j))],
        out_specs=pl.BlockSpec((tm, tn), lambda i, j: (i, j)),
        compiler_params=_cparams(2),
    )(x, w)


def _merge_kernel(a_ref, r_ref, g0_ref, g1_ref, wa_ref, wb_ref, o_ref):
    ya = jnp.dot(a_ref[...], wa_ref[...].astype(BF16), preferred_element_type=F32)
    yb = jnp.dot(r_ref[...], wb_ref[...].astype(BF16), preferred_element_type=F32)
    g0 = jax.nn.sigmoid(g0_ref[...].astype(F32))
    g1 = jax.nn.sigmoid(g1_ref[...].astype(F32))
    o_ref[...] = (g0 * ya + g1 * yb).astype(o_ref.dtype)


def _merge(a, rn, proj, gate_col0, w_a, w_b, layer, tn):
    m, ka = a.shape
    kb = rn.shape[1]
    d = w_a.shape[2]
    tm = _matmul_rows(m)
    assert d % tn == 0 and gate_col0 % tn == 0
    c0 = gate_col0 // tn
    c1 = (gate_col0 + d) // tn
    return pl.pallas_call(
        _merge_kernel,
        out_shape=jax.ShapeDtypeStruct((m, d), BF16),
        grid=(m // tm, d // tn),
        in_specs=[pl.BlockSpec((tm, ka), lambda i, j: (i, 0)),
                  pl.BlockSpec((tm, kb), lambda i, j: (i, 0)),
                  pl.BlockSpec((tm, tn), lambda i, j: (i, c0 + j)),
                  pl.BlockSpec((tm, tn), lambda i, j: (i, c1 + j)),
                  pl.BlockSpec((None, ka, tn), lambda i, j: (layer, 0, j)),
                  pl.BlockSpec((None, kb, tn), lambda i, j: (layer, 0, j))],
        out_specs=pl.BlockSpec((tm, tn), lambda i, j: (i, j)),
        compiler_params=_cparams(2),
    )(a, rn, proj, proj, w_a, w_b)


def _rope_tables(pos, rot_dim, head_dim, base):
    half = rot_dim // 2
    inv = 1.0 / (base ** (jnp.arange(half, dtype=F32) * (2.0 / rot_dim)))
    ang = pos.astype(F32)[:, None] * inv[None, :]
    cos, sin = jnp.cos(ang), jnp.sin(ang)
    t = pos.shape[0]
    rest = head_dim - rot_dim
    cos_h = jnp.concatenate([cos, cos, jnp.ones((t, rest), F32)], axis=1)
    sin_lo = jnp.concatenate([jnp.zeros((t, half), F32), sin, jnp.zeros((t, rest), F32)], axis=1)
    sin_hi = jnp.concatenate([-sin, jnp.zeros((t, half + rest), F32)], axis=1)
    reps = LANES // head_dim
    tile = lambda a: jnp.tile(a, (1, reps))
    return tile(cos_h), tile(sin_lo), tile(sin_hi)


def _rope_lanes(x, cos, sin_lo, sin_hi, half):
    return (x * cos + pltpu.roll(x, half, axis=1) * sin_lo
            + pltpu.roll(x, LANES - half, axis=1) * sin_hi)


def _rope_wide(x, cos, sin_lo, sin_hi, half):
    cols = [_rope_lanes(x[:, c:c + LANES], cos, sin_lo, sin_hi, half)
            for c in range(0, x.shape[1], LANES)]
    return cols[0] if len(cols) == 1 else jnp.concatenate(cols, axis=1)


def _dot_nt(a, b):
    return lax.dot_general(a, b, (((1,), (1,)), ((), ())), preferred_element_type=F32)


def _dot_tn(a, b):
    return lax.dot_general(a, b, (((0,), (0,)), ((), ())), preferred_element_type=F32)


def _sink_softmax_pv(score_parts, value_parts, sink_col):
    m = sink_col
    for s in score_parts:
        m = jnp.maximum(m, jnp.max(s, axis=1, keepdims=True))
    den = jnp.exp(sink_col - m)
    acc = None
    for s, v in zip(score_parts, value_parts):
        p = jnp.exp(s - m)
        den = den + jnp.sum(p, axis=1, keepdims=True)
        pv = jnp.dot(p.astype(BF16), v, preferred_element_type=F32)
        acc = pv if acc is None else acc + pv
    return acc * (1.0 / den)


def _swa_prompt_kernel(sink_ref, q_ref, kc_ref, kp_ref, vc_ref, vp_ref,
                       cc_ref, lc_ref, hc_ref, cp_ref, lp_ref, hp_ref, bias_ref,
                       a_ref, nk_ref, nv_ref, *, layer, n_heads, n_kv, hd, half):
    w = q_ref.shape[0]
    grp = n_heads // n_kv
    tabs_c = (cc_ref[...], lc_ref[...], hc_ref[...])
    tabs_p = (cp_ref[...], lp_ref[...], hp_ref[...])
    q = (_rope_wide(q_ref[...].astype(F32), *tabs_c, half) * (hd ** -0.5)).astype(BF16)
    kc_f = _rope_wide(kc_ref[...].astype(F32), *tabs_c, half)
    kc = kc_f.astype(BF16)
    kp = _rope_wide(kp_ref[...].astype(F32), *tabs_p, half).astype(BF16)
    vc, vp = vc_ref[...], vp_ref[...]
    bias = bias_ref[...]
    bias_g = jnp.concatenate([bias] * grp, axis=0)
    outs = []
    for g in range(n_kv):
        ks = slice(g * hd, (g + 1) * hd)
        qg = jnp.concatenate([q[:, (g * grp + j) * hd:(g * grp + j + 1) * hd]
                              for j in range(grp)], axis=0)
        kk = jnp.concatenate([kp[:, ks], kc[:, ks]], axis=0)
        vv = jnp.concatenate([vp[:, ks], vc[:, ks]], axis=0)
        sink = jnp.concatenate([jnp.full((w, 1), sink_ref[layer, g * grp + j], F32)
                                for j in range(grp)], axis=0)
        s = _dot_nt(qg, kk) + bias_g
        o = _sink_softmax_pv([s], [vv], sink)
        outs += [o[j * w:(j + 1) * w] for j in range(grp)]
    a_ref[...] = jnp.concatenate(outs, axis=1).astype(a_ref.dtype)

    @pl.when(pl.program_id(1) == pl.num_programs(1) - 1)
    def _():
        nk_ref[...] = kc_f
        nv_ref[...] = vc.astype(F32)


def _swa_prompt(proj, sinks, layer, bp, seq_len, n_rows, n_heads, n_kv, hd, tabs, bias):
    w = WINDOW
    nb = seq_len // w
    qw, kvw = n_heads * hd, n_kv * hd
    assert qw % kvw == 0
    kcol, vcol = qw // kvw, qw // kvw + 1
    cur = lambda b, n: b * nb + n
    prev = lambda b, n: b * nb + jnp.maximum(n - 1, 0)
    tab_cur = pl.BlockSpec((w, LANES), lambda b, n: (n, 0))
    tab_prev = pl.BlockSpec((w, LANES), lambda b, n: (jnp.maximum(n - 1, 0), 0))
    kern = functools.partial(_swa_prompt_kernel, layer=layer, n_heads=n_heads, n_kv=n_kv,
                             hd=hd, half=hd // 8)
    return pl.pallas_call(
        kern,
        out_shape=[jax.ShapeDtypeStruct((n_rows, qw), BF16),
                   jax.ShapeDtypeStruct((bp, w, kvw), F32),
                   jax.ShapeDtypeStruct((bp, w, kvw), F32)],
        grid=(bp, nb),
        in_specs=[pl.BlockSpec(memory_space=pltpu.SMEM),
                  pl.BlockSpec((w, qw), lambda b, n: (cur(b, n), 0)),
                  pl.BlockSpec((w, kvw), lambda b, n: (cur(b, n), kcol)),
                  pl.BlockSpec((w, kvw), lambda b, n: (prev(b, n), kcol)),
                  pl.BlockSpec((w, kvw), lambda b, n: (cur(b, n), vcol)),
                  pl.BlockSpec((w, kvw), lambda b, n: (prev(b, n), vcol)),
                  tab_cur, tab_cur, tab_cur, tab_prev, tab_prev, tab_prev,
                  pl.BlockSpec((None, w, 2 * w), lambda b, n: (jnp.minimum(n, 1), 0, 0))],
        out_specs=[pl.BlockSpec((w, qw), lambda b, n: (cur(b, n), 0)),
                   pl.BlockSpec((None, w, kvw), lambda b, n: (b, 0, 0)),
                   pl.BlockSpec((None, w, kvw), lambda b, n: (b, 0, 0))],
        compiler_params=_cparams(2),
    )(sinks, proj, proj, proj, proj, proj, *tabs, *tabs, bias)


def _swa_sample_kernel(sink_ref, a_in_ref, q_ref, k_ref, v_ref, kb_ref, vb_ref,
                       c_ref, l_ref, h_ref, bias_c_ref, bias_n_ref,
                       a_ref, nk_ref, nv_ref, *, layer, n_heads, n_kv, hd, half):
    del a_in_ref
    rows = q_ref.shape[0]
    grp = n_heads // n_kv
    tabs = (c_ref[...], l_ref[...], h_ref[...])
    q = (_rope_wide(q_ref[...].astype(F32), *tabs, half) * (hd ** -0.5)).astype(BF16)
    kn_f = _rope_wide(k_ref[...].astype(F32), *tabs, half)
    kn = kn_f.astype(BF16)
    vn = v_ref[...]
    kb = kb_ref[...].astype(BF16)
    vb = vb_ref[...].astype(BF16)
    bias_c = bias_c_ref[...]
    bias_n = bias_n_ref[...]
    outs = []
    for g in range(n_kv):
        ks = slice(g * hd, (g + 1) * hd)
        qg = jnp.concatenate([q[:, (g * grp + j) * hd:(g * grp + j + 1) * hd]
                              for j in range(grp)], axis=0)
        sink = jnp.concatenate([jnp.full((rows, 1), sink_ref[layer, g * grp + j], F32)
                                for j in range(grp)], axis=0)
        s_c = _dot_nt(qg, kb[:, ks]) + bias_c
        s_n = _dot_nt(qg, kn[:, ks]) + bias_n
        o = _sink_softmax_pv([s_c, s_n], [vb[:, ks], vn[:, ks]], sink)
        outs += [o[j * rows:(j + 1) * rows] for j in range(grp)]
    a_ref[...] = jnp.concatenate(outs, axis=1).astype(a_ref.dtype)
    nk_ref[...] = kn_f
    nv_ref[...] = vn.astype(F32)


def _swa_sample(proj, a, sinks, cache_k, cache_v, layer, row0, bs, ts, n_heads, n_kv, hd,
                tabs, bias_c, bias_n):
    sg = SAMPLE_SEQS_PER_STEP
    rows = sg * ts
    wb = cache_k.shape[1] // bs
    qw, kvw = n_heads * hd, n_kv * hd
    assert bs % sg == 0 and row0 % rows == 0 and rows % BF16_SUBLANES == 0
    r0 = row0 // rows
    kcol, vcol = qw // kvw, qw // kvw + 1
    full = lambda shape: pl.BlockSpec(shape, lambda s: (0,) * len(shape))
    kern = functools.partial(_swa_sample_kernel, layer=layer, n_heads=n_heads, n_kv=n_kv,
                             hd=hd, half=hd // 8)
    return pl.pallas_call(
        kern,
        out_shape=[jax.ShapeDtypeStruct(a.shape, a.dtype),
                   jax.ShapeDtypeStruct((bs * ts, kvw), F32),
                   jax.ShapeDtypeStruct((bs * ts, kvw), F32)],
        grid=(bs // sg,),
        in_specs=[pl.BlockSpec(memory_space=pltpu.SMEM),
                  pl.BlockSpec(memory_space=pl.ANY),
                  pl.BlockSpec((rows, qw), lambda s: (r0 + s, 0)),
                  pl.BlockSpec((rows, kvw), lambda s: (r0 + s, kcol)),
                  pl.BlockSpec((rows, kvw), lambda s: (r0 + s, vcol)),
                  pl.BlockSpec((None, sg * wb, kvw), lambda s: (layer, s, 0)),
                  pl.BlockSpec((None, sg * wb, kvw), lambda s: (layer, s, 0)),
                  full((rows, LANES)), full((rows, LANES)), full((rows, LANES)),
                  full(bias_c.shape), full(bias_n.shape)],
        out_specs=[pl.BlockSpec((rows, qw), lambda s: (r0 + s, 0)),
                   pl.BlockSpec((rows, kvw), lambda s: (s, 0)),
                   pl.BlockSpec((rows, kvw), lambda s: (s, 0))],
        input_output_aliases={1: 0},
        compiler_params=_cparams(1),
    )(sinks, a, proj, proj, proj, cache_k, cache_v, *tabs, bias_c, bias_n)


def _ret_head(q_raw, k_raw, v, g_raw, cos, sin, scale):
    half = LANES // 2
    q = q_raw * cos + pltpu.roll(q_raw, half, axis=1) * sin
    k = (k_raw * cos + pltpu.roll(k_raw, half, axis=1) * sin) * scale
    return q, k


def _ret_finish(o, g_raw):
    n = o * lax.rsqrt(jnp.mean(o * o, axis=-1, keepdims=True) + EPS)
    return n * _silu(g_raw)


def _ret_prompt_kernel(q_ref, k_ref, v_ref, g_ref, cos_ref, sin_ref,
                       intra_ref, qd_ref, kd_ref, cd_ref,
                       rn_ref, st_ref, s_ref, *, heads, scale):
    c = pl.program_id(2)

    @pl.when(c == 0)
    def _():
        s_ref[...] = jnp.zeros_like(s_ref)

    cos, sin = cos_ref[...], sin_ref[...]
    for hh in range(heads):
        sl = slice(hh * LANES, (hh + 1) * LANES)
        q, k = _ret_head(q_ref[:, sl].astype(F32), k_ref[:, sl].astype(F32), None, None,
                         cos, sin, scale)
        v = v_ref[:, sl]
        att = _dot_nt(q.astype(BF16), k.astype(BF16)) * intra_ref[hh]
        s0 = s_ref[hh]
        o = (jnp.dot(att.astype(BF16), v, preferred_element_type=F32)
             + jnp.dot((q * qd_ref[hh]).astype(BF16), s0.astype(BF16),
                       preferred_element_type=F32))
        s_ref[hh] = s0 * cd_ref[hh] + _dot_tn((k * kd_ref[hh]).astype(BF16), v)
        rn_ref[:, sl] = _ret_finish(o, g_ref[:, sl].astype(F32)).astype(rn_ref.dtype)

    @pl.when(c == pl.num_programs(2) - 1)
    def _():
        st_ref[...] = s_ref[...]


def _ret_prompt(proj, layer, bp, seq_len, n_rows, col0, n_heads, tabs, dec):
    del layer
    ch = RET_CHUNK
    nc = seq_len // ch
    hw = n_heads * LANES
    hg = 4 if n_heads % 4 == 0 else n_heads
    bw = hg * LANES
    nhalf = n_heads // hg
    assert col0 % bw == 0 and hw % bw == 0
    cb = lambda k: (col0 + k * hw) // bw
    row = lambda b, hf, c: b * nc + c
    inspec = lambda k: pl.BlockSpec((ch, bw), lambda b, hf, c: (row(b, hf, c), cb(k) + hf))
    tab = pl.BlockSpec((ch, LANES), lambda b, hf, c: (c, 0))
    dspec = pl.BlockSpec((hg, ch, LANES), lambda b, hf, c: (hf, 0, 0))
    kern = functools.partial(_ret_prompt_kernel, heads=hg, scale=LANES ** -0.5)
    return pl.pallas_call(
        kern,
        out_shape=[jax.ShapeDtypeStruct((n_rows, hw), BF16),
                   jax.ShapeDtypeStruct((bp, n_heads, LANES, LANES), F32)],
        grid=(bp, nhalf, nc),
        in_specs=[inspec(0), inspec(1), inspec(2), inspec(3), tab, tab,
                  dspec, dspec, dspec, dspec],
        out_specs=[pl.BlockSpec((ch, bw), lambda b, hf, c: (row(b, hf, c), hf)),
                   pl.BlockSpec((None, hg, LANES, LANES), lambda b, hf, c: (b, hf, 0, 0))],
        scratch_shapes=[pltpu.VMEM((hg, LANES, LANES), F32)],
        compiler_params=_cparams(3),
    )(proj, proj, proj, proj, *tabs, *dec)


def _ret_sample_kernel(rn_in_ref, q_ref, k_ref, v_ref, g_ref, s_in_ref, cos_ref, sin_ref,
                       intra_ref, qd_ref, kd_ref, cd_ref, sel_ref,
                       rn_ref, s_out_ref, *, heads, seqs, scale):
    del rn_in_ref
    cos, sin = cos_ref[...], sin_ref[...]
    for hh in range(heads):
        sl = slice(hh * LANES, (hh + 1) * LANES)
        q, k = _ret_head(q_ref[:, sl].astype(F32), k_ref[:, sl].astype(F32), None, None,
                         cos, sin, scale)
        v = v_ref[:, sl]
        att = _dot_nt(q.astype(BF16), k.astype(BF16)) * intra_ref[hh]
        o = jnp.dot(att.astype(BF16), v, preferred_element_type=F32)
        qd = (q * qd_ref[hh]).astype(BF16)
        kd = k * kd_ref[hh]
        for s in range(seqs):
            sel = sel_ref[s]
            s0 = s_in_ref[s, hh]
            o = o + sel * jnp.dot(qd, s0.astype(BF16), preferred_element_type=F32)
            s_out_ref[s, hh] = s0 * cd_ref[hh] + _dot_tn((kd * sel).astype(BF16), v)
        rn_ref[:, sl] = _ret_finish(o, g_ref[:, sl].astype(F32)).astype(rn_ref.dtype)


def _ret_sample(proj, rn, state, layer, row0, bs, ts, col0, n_heads, tabs, dec, sel):
    sg = SAMPLE_SEQS_PER_STEP
    rows = sg * ts
    hw = n_heads * LANES
    hg = 4 if n_heads % 4 == 0 else n_heads
    bw = hg * LANES
    nhalf = n_heads // hg
    assert bs % sg == 0 and row0 % rows == 0 and col0 % bw == 0
    r0 = row0 // rows
    cb = lambda k: (col0 + k * hw) // bw
    inspec = lambda k: pl.BlockSpec((rows, bw), lambda s, hf: (r0 + s, cb(k) + hf))
    tab = pl.BlockSpec((rows, LANES), lambda s, hf: (0, 0))
    dspec = lambda a: pl.BlockSpec((hg,) + a.shape[1:], lambda s, hf: (hf, 0, 0))
    st_spec = pl.BlockSpec((None, sg, hg, LANES, LANES), lambda s, hf: (layer, s, hf, 0, 0))
    kern = functools.partial(_ret_sample_kernel, heads=hg, seqs=sg, scale=LANES ** -0.5)
    return pl.pallas_call(
        kern,
        out_shape=[jax.ShapeDtypeStruct(rn.shape, rn.dtype),
                   jax.ShapeDtypeStruct(state.shape[1:], F32)],
        grid=(bs // sg, nhalf),
        in_specs=[pl.BlockSpec(memory_space=pl.ANY),
                  inspec(0), inspec(1), inspec(2), inspec(3), st_spec, tab, tab,
                  dspec(dec[0]), dspec(dec[1]), dspec(dec[2]), dspec(dec[3]),
                  pl.BlockSpec(sel.shape, lambda s, hf: (0, 0, 0))],
        out_specs=[pl.BlockSpec((rows, bw), lambda s, hf: (r0 + s, hf)),
                   pl.BlockSpec((sg, hg, LANES, LANES), lambda s, hf: (s, hf, 0, 0))],
        input_output_aliases={0: 0},
        compiler_params=_cparams(2),
    )(rn, proj, proj, proj, proj, state, *tabs, *dec, sel)


def _ret_decay_tables(n_heads, chunk, rows):
    log_g = jnp.log(1.0 - jnp.exp2(-5.0 - jnp.arange(n_heads, dtype=F32)))
    i = jnp.arange(chunk, dtype=F32)
    diff = i[:, None] - i[None, :]
    intra = jnp.where(diff >= 0, jnp.exp(jnp.maximum(diff, 0.0) * log_g[:, None, None]), 0.0)
    q_dec = jnp.exp((i + 1.0) * log_g[:, None])
    k_dec = jnp.exp((chunk - 1.0 - i) * log_g[:, None])
    c_dec = jnp.exp(chunk * log_g)
    reps = rows // chunk
    same = jnp.kron(jnp.eye(reps, dtype=F32), jnp.ones((chunk, chunk), F32))
    intra_t = jnp.tile(intra, (1, reps, reps)) * same[None]
    qd_t = jnp.broadcast_to(jnp.tile(q_dec, (1, reps))[:, :, None], (n_heads, rows, LANES))
    kd_t = jnp.broadcast_to(jnp.tile(k_dec, (1, reps))[:, :, None], (n_heads, rows, LANES))
    cd_t = jnp.broadcast_to(c_dec[:, None, None], (n_heads, LANES, LANES))
    return intra_t, qd_t, kd_t, cd_t


def _ret_rope_tables(pos):
    half = LANES // 2
    inv = 1.0 / (RET_ROT_BASE ** (jnp.arange(half, dtype=F32) * (2.0 / LANES)))
    ang = pos.astype(F32)[:, None] * inv[None, :]
    cos, sin = jnp.cos(ang), jnp.sin(ang)
    return jnp.concatenate([cos, cos], axis=1), jnp.concatenate([-sin, sin], axis=1)


def _moe_plan(route_out, n_tok, n_exp, tm, n_tiles, plane_rows):
    e_id = route_out[:, 0:2].astype(I32)
    e_w = route_out[:, 2:4]
    n_asg = 2 * n_tok
    e_flat = e_id.reshape(n_asg)
    w_flat = e_w.reshape(n_asg)
    order = jnp.argsort(e_flat, stable=True).astype(I32)
    cnt = jnp.sum((e_flat[:, None] == jnp.arange(n_exp, dtype=I32)[None, :]).astype(I32), axis=0)
    cnt_pad = ((cnt + tm - 1) // tm) * tm
    seg_end = jnp.cumsum(cnt_pad)
    seg_start = seg_end - cnt_pad
    cstart = jnp.cumsum(cnt) - cnt
    n_active = (seg_end[-1] // tm).astype(I32)
    tile_start = jnp.arange(n_tiles, dtype=I32) * tm
    te = jnp.sum((seg_end[None, :] <= tile_start[:, None]).astype(I32), axis=1)
    te = jnp.minimum(te, n_exp - 1)
    last = jnp.take(te, n_active - 1)
    te = jnp.where(jnp.arange(n_tiles) < n_active, te, last).astype(I32)
    row = jnp.arange(n_tiles * tm, dtype=I32)
    row_e = jnp.repeat(te, tm)
    off = row - jnp.take(seg_start, row_e)
    valid = (off < jnp.take(cnt, row_e)) & (row < seg_end[-1])
    src = jnp.take(order, jnp.clip(jnp.take(cstart, row_e) + off, 0, n_asg - 1))
    tok = jnp.where(valid, src // 2, 0).astype(I32)
    trash = 2 * plane_rows + (row % tm)
    dst = jnp.where(valid, (src % 2) * plane_rows + src // 2, trash).astype(I32)
    roww = jnp.where(valid, jnp.take(w_flat, src), 0.0)
    roww = jnp.broadcast_to(roww[:, None], (n_tiles * tm, LANES))
    return te, n_active.reshape(1), tok, dst, roww


def _moe_kernel(te_ref, na_ref, tok_ref, dst_ref,
                h_hbm, roww_ref, wg_ref, wu_ref, wd_ref,
                y_hbm,
                gbuf, obuf, wg_bf, wu_bf, wd_bf, gsem, ssem):
    i = pl.program_id(0)
    n_act = na_ref[0]
    tm = obuf.shape[0]

    def gather_start(tile, slot):
        base = tile * tm

        def body(r, carry):
            t = tok_ref[base + r]
            pltpu.make_async_copy(h_hbm.at[pl.ds(t, 1)], gbuf.at[slot, pl.ds(r, 1)],
                                  gsem.at[slot]).start()
            return carry

        lax.fori_loop(0, tm, body, 0, unroll=DMA_ISSUE_UNROLL)

    def gather_wait(slot):
        pltpu.make_async_copy(h_hbm.at[pl.ds(0, tm)], gbuf.at[slot], gsem.at[slot]).wait()

    def scatter_start(tile):
        base = tile * tm

        def body(r, carry):
            d = dst_ref[base + r]
            pltpu.make_async_copy(obuf.at[pl.ds(r, 1)], y_hbm.at[pl.ds(d, 1)],
                                  ssem.at[0]).start()
            return carry

        lax.fori_loop(0, tm, body, 0, unroll=DMA_ISSUE_UNROLL)

    def scatter_wait():
        pltpu.make_async_copy(obuf, y_hbm.at[pl.ds(0, tm)], ssem.at[0]).wait()

    @pl.when(i < n_act)
    def _():
        slot = lax.rem(i, 2)

        @pl.when(i == 0)
        def _():
            gather_start(0, 0)

        gather_wait(slot)

        @pl.when(i + 1 < n_act)
        def _():
            gather_start(i + 1, 1 - slot)

        expert_changed = jnp.logical_or(i == 0, te_ref[i] != te_ref[jnp.maximum(i - 1, 0)])

        @pl.when(expert_changed)
        def _():
            wg_bf[...] = wg_ref[...].astype(BF16)
            wu_bf[...] = wu_ref[...].astype(BF16)
            wd_bf[...] = wd_ref[...].astype(BF16)

        x = gbuf[slot].astype(BF16)
        g = jnp.dot(x, wg_bf[...], preferred_element_type=F32)
        u = jnp.dot(x, wu_bf[...], preferred_element_type=F32)
        roww = roww_ref[...]
        hid = _silu(g) * u * jnp.concatenate([roww] * (g.shape[1] // LANES), axis=1)
        y = jnp.dot(hid.astype(BF16), wd_bf[...], preferred_element_type=F32)

        @pl.when(i > 0)
        def _():
            scatter_wait()

        obuf[...] = y
        scatter_start(i)

        @pl.when(i == n_act - 1)
        def _():
            scatter_wait()


def _moe(h, route_out, w_g, w_u, w_d, layer, plane_rows):
    n_tok, d = h.shape
    n_exp, _, ff = w_g.shape[1:]
    tm = MOE_TILE_ROWS
    n_tiles = -(-2 * n_tok // tm) + n_exp
    assert ff % LANES == 0 and plane_rows >= max(n_tok, tm)
    te, n_act, tok, dst, roww = _moe_plan(route_out, n_tok, n_exp, tm, n_tiles, plane_rows)
    grid_spec = pltpu.PrefetchScalarGridSpec(
        num_scalar_prefetch=4,
        grid=(n_tiles,),
        in_specs=[pl.BlockSpec(memory_space=pl.ANY),
                  pl.BlockSpec((tm, LANES), lambda i, te, na, tok, dst: (i, 0)),
                  pl.BlockSpec((None, None, d, ff), lambda i, te, na, tok, dst: (layer, te[i], 0, 0)),
                  pl.BlockSpec((None, None, d, ff), lambda i, te, na, tok, dst: (layer, te[i], 0, 0)),
                  pl.BlockSpec((None, None, ff, d), lambda i, te, na, tok, dst: (layer, te[i], 0, 0))],
        out_specs=pl.BlockSpec(memory_space=pl.ANY),
        scratch_shapes=[pltpu.VMEM((2, tm, d), F32),
                        pltpu.VMEM((tm, d), F32),
                        pltpu.VMEM((d, ff), BF16),
                        pltpu.VMEM((d, ff), BF16),
                        pltpu.VMEM((ff, d), BF16),
                        pltpu.SemaphoreType.DMA((2,)),
                        pltpu.SemaphoreType.DMA((1,))],
    )
    y = pl.pallas_call(
        _moe_kernel,
        out_shape=jax.ShapeDtypeStruct((3 * plane_rows, d), F32),
        grid_spec=grid_spec,
        compiler_params=pltpu.CompilerParams(dimension_semantics=("arbitrary",),
                                             vmem_limit_bytes=VMEM_LIMIT_BYTES,
                                             has_side_effects=True),
    )(te, n_act, tok, dst, h, roww, w_g, w_u, w_d)
    return y.reshape(3, plane_rows, d)


def _swa_prompt_bias(w):
    r = jnp.arange(w)[:, None]
    c = jnp.arange(2 * w)[None, :]
    band = (c <= r + w) & (c > r)
    first = band & (c >= w)
    return jnp.where(jnp.stack([first, band]), 0.0, -jnp.inf).astype(F32)


def _swa_sample_bias(sg, ts, wb, grp):
    rho = jnp.arange(sg * ts)
    seq_q, t = rho // ts, rho % ts
    cc = jnp.arange(sg * wb)
    ok_c = (cc[None, :] // wb == seq_q[:, None]) & (cc[None, :] % wb - wb > t[:, None] - WINDOW)
    cn = jnp.arange(sg * ts)
    ok_n = (cn[None, :] // ts == seq_q[:, None]) & (cn[None, :] % ts <= t[:, None])
    to_bias = lambda ok: jnp.tile(jnp.where(ok, 0.0, -jnp.inf).astype(F32), (grp, 1))
    return to_bias(ok_c), to_bias(ok_n)


def kernel(x_prompt, x_sample, cache_win_k, cache_win_v, state_ret, c_prompt, c_sample,
           w_ada, b_ada, g_mix, g_ffn, w_in, attn_sinks, w_branch_attn, w_branch_ret, w_out,
           w_route_group, w_route_expert, w_exp_gate, w_exp_up, w_exp_down, g_final):
    bp, seq_len, d = x_prompt.shape
    bs, ts, _ = x_sample.shape
    depth = w_in.shape[0]
    n_heads = attn_sinks.shape[1]
    _, _, wb, n_kv, hd = cache_win_k.shape
    ret_heads, ret_hd = state_ret.shape[2], state_ret.shape[3]
    n_groups = w_route_group.shape[2]
    n_exp = w_route_expert.shape[2]
    per_group = n_exp // n_groups
    qw, kvw, rw = n_heads * hd, n_kv * hd, ret_heads * ret_hd
    assert ret_hd == LANES and LANES % hd == 0 and wb == WINDOW
    assert seq_len % WINDOW == 0 and seq_len % RET_CHUNK == 0 and ts < RET_CHUNK
    n_p, n_s = bp * seq_len, bs * ts
    n_tot = n_p + n_s
    ret_col0 = qw + 2 * kvw
    gate_col0 = ret_col0 + 4 * rw
    proj_tn = 512
    te = min(ELEM_TILE_ROWS, seq_len)
    plane_rows = -(-n_tot // te) * te

    r_pad = -(-(bp + bs) // BF16_SUBLANES) * BF16_SUBLANES
    c_all = jnp.concatenate([c_prompt, c_sample, jnp.zeros((r_pad - bp - bs, d), F32)], axis=0)
    mod = _ada(c_all, w_ada, b_ada)
    mod_p = mod[:, :bp].reshape(depth, bp, 1, 6 * d)
    mod_s = jnp.repeat(mod[:, bp:bp + bs], ts, axis=1)

    pos_p = jnp.arange(seq_len, dtype=I32)
    pos_s = PAST_LEN + jnp.arange(ts, dtype=I32)
    sg = SAMPLE_SEQS_PER_STEP
    swa_tabs_p = _rope_tables(pos_p, hd // 4, hd, ROPE_THETA)
    swa_tabs_s = tuple(jnp.tile(t, (sg, 1)) for t in _rope_tables(pos_s, hd // 4, hd, ROPE_THETA))
    ret_tabs_p = _ret_rope_tables(pos_p)
    ret_tabs_s = tuple(jnp.tile(t, (sg, 1)) for t in _ret_rope_tables(pos_s))
    dec_p = _ret_decay_tables(ret_heads, RET_CHUNK, RET_CHUNK)
    dec_s = _ret_decay_tables(ret_heads, ts, sg * ts)
    sel_s = jnp.broadcast_to(
        (jnp.arange(sg * ts)[None, :] // ts == jnp.arange(sg)[:, None]).astype(F32)[:, :, None],
        (sg, sg * ts, LANES))
    bias_p = _swa_prompt_bias(WINDOW)
    bias_sc, bias_sn = _swa_sample_bias(sg, ts, wb, n_heads // n_kv)

    w_route = jnp.concatenate(
        [w_route_group, w_route_expert,
         jnp.zeros((depth, d, ROUTER_LANES - n_groups - n_exp), F32)], axis=2)
    cache_k = cache_win_k.reshape(depth, bs * wb, kvw)
    cache_v = cache_win_v.reshape(depth, bs * wb, kvw)

    xp = x_prompt.reshape(n_p, d)
    xs = x_sample.reshape(n_s, d)
    moe_out = None
    outs = {k: [] for k in ("pk", "pv", "ps", "sk", "sv", "ss")}
    for l in range(depth):
        adds = [] if moe_out is None else [(moe_out, 0), (moe_out, 1)]
        gate = None if moe_out is None else (mod_p, mod_s, l - 1, 5)
        res = _norm_call(xp, xs, seq_len, adds, gate, g_mix[l].reshape(1, d),
                         (mod_p, mod_s, l, 1, 0), emit_x=moe_out is not None, h_dtype=BF16)
        if moe_out is not None:
            xp, xs, h = res
        else:
            (h,) = res
        proj = _matmul(h, w_in, l, proj_tn)

        a, pk, pv = _swa_prompt(proj, attn_sinks, l, bp, seq_len, n_tot, n_heads, n_kv, hd,
                                swa_tabs_p, bias_p)
        a, k_new, v_new = _swa_sample(proj, a, attn_sinks, cache_k, cache_v, l, n_p, bs, ts,
                                      n_heads, n_kv, hd, swa_tabs_s, bias_sc, bias_sn)
        rn, st_p = _ret_prompt(proj, l, bp, seq_len, n_tot, ret_col0, ret_heads, ret_tabs_p, dec_p)
        rn, st_s = _ret_sample(proj, rn, state_ret, l, n_p, bs, ts, ret_col0, ret_heads,
                               ret_tabs_s, dec_s, sel_s)
        merged = _merge(a, rn, proj, gate_col0, w_branch_attn, w_branch_ret, l, min(proj_tn, d))
        mix = _matmul(merged, w_out, l, min(proj_tn, d))

        outs["pk"].append(pk.reshape(bp, wb, n_kv, hd))
        outs["pv"].append(pv.reshape(bp, wb, n_kv, hd))
        outs["ps"].append(st_p)
        outs["sk"].append(jnp.concatenate([cache_win_k[l][:, ts:], k_new.reshape(bs, ts, n_kv, hd)], axis=1))
        outs["sv"].append(jnp.concatenate([cache_win_v[l][:, ts:], v_new.reshape(bs, ts, n_kv, hd)], axis=1))
        outs["ss"].append(st_s)

        xp, xs, h2, route_out = _norm_call(
            xp, xs, seq_len, [(mix, None)], (mod_p, mod_s, l, 2), g_ffn[l].reshape(1, d),
            (mod_p, mod_s, l, 4, 3), emit_x=True, h_dtype=F32,
            route=(w_route[l], n_groups, per_group))
        moe_out = _moe(h2, route_out, w_exp_gate, w_exp_up, w_exp_down, l, plane_rows)

    yp, ys = _norm_call(xp, xs, seq_len, [(moe_out, 0), (moe_out, 1)],
                        (mod_p, mod_s, depth - 1, 5), g_final.reshape(1, d), None,
                        emit_x=False, final=True)
    stack = lambda k: jnp.stack(outs[k])
    return (yp.reshape(bp, seq_len, d), ys.reshape(bs, ts, d),
            stack("pk"), stack("pv"), stack("ps"), stack("sk"), stack("sv"), stack("ss"))
```

```python
import functools

import jax
import jax.numpy as jnp
from jax import lax
from jax.experimental import pallas as pl
from jax.experimental.pallas import tpu as pltpu

F32 = jnp.float32
BF16 = jnp.bfloat16
I32 = jnp.int32

PAST_LEN = 16384
WINDOW = 128
ROPE_THETA = 500000.0
RET_CHUNK = 128
RET_ROT_BASE = 10000.0
EPS = 1e-6

LANES = 128
BF16_SUBLANES = 16
VMEM_LIMIT_BYTES = 56 * 1024 * 1024
ELEM_TILE_ROWS = 512
MATMUL_MAX_ROWS = 2048
SAMPLE_SEQS_PER_STEP = 8
MOE_TILE_ROWS = 256
MOE_OUT_CHUNKS = 4
ROUTER_LANES = 128


def _cparams(n_axes):
    return pltpu.CompilerParams(dimension_semantics=("arbitrary",) * n_axes,
                                vmem_limit_bytes=VMEM_LIMIT_BYTES)


def _matmul_rows(n):
    for d in range(min(n, MATMUL_MAX_ROWS), 0, -1):
        if n % d == 0 and d % BF16_SUBLANES == 0:
            return d
    raise ValueError(f"no row tile for {n} rows")


def _silu(x):
    return x * jax.nn.sigmoid(x)


def _ada_kernel(c_ref, w_ref, b_ref, o_ref):
    s = _silu(c_ref[...]).astype(BF16)
    o_ref[0] = jnp.dot(s, w_ref[0].astype(BF16), preferred_element_type=F32) + b_ref[0]


def _ada(c_all, w_ada, b_ada):
    depth, d, n6 = w_ada.shape
    r = c_all.shape[0]
    tn = min(1024, d)
    assert n6 % tn == 0
    return pl.pallas_call(
        _ada_kernel,
        out_shape=jax.ShapeDtypeStruct((depth, r, n6), F32),
        grid=(depth, n6 // tn),
        in_specs=[pl.BlockSpec((r, d), lambda l, j: (0, 0)),
                  pl.BlockSpec((1, d, tn), lambda l, j: (l, 0, j)),
                  pl.BlockSpec((1, 1, tn), lambda l, j: (l, 0, j))],
        out_specs=pl.BlockSpec((1, r, tn), lambda l, j: (l, 0, j)),
        name="adaln_mod",
        compiler_params=_cparams(2),
    )(c_all, w_ada, b_ada.reshape(depth, 1, n6))


def _route(logits, n_groups, per_group):
    lane = lax.broadcasted_iota(I32, logits.shape, 1).astype(F32)
    big = float(ROUTER_LANES)
    ninf = -jnp.inf
    gl = jnp.where(lane < n_groups, logits, ninf)
    gmax = jnp.max(gl, axis=1, keepdims=True)
    gsel = jnp.min(jnp.where(gl == gmax, lane, big), axis=1, keepdims=True)
    gden = jnp.sum(jnp.exp(gl - gmax), axis=1, keepdims=True)
    gw = 1.0 / gden
    lo = n_groups + gsel * per_group
    el = jnp.where(jnp.logical_and(lane >= lo, lane < lo + per_group), logits, ninf)
    v1 = jnp.max(el, axis=1, keepdims=True)
    i1 = jnp.min(jnp.where(el == v1, lane, big), axis=1, keepdims=True)
    el2 = jnp.where(lane == i1, ninf, el)
    v2 = jnp.max(el2, axis=1, keepdims=True)
    i2 = jnp.min(jnp.where(el2 == v2, lane, big), axis=1, keepdims=True)
    t = jnp.exp(v2 - v1)
    w1 = gw / (1.0 + t)
    w2 = gw * t / (1.0 + t)
    out = jnp.where(lane == 0, i1 - n_groups,
                    jnp.where(lane == 1, i2 - n_groups,
                              jnp.where(lane == 2, w1, jnp.where(lane == 3, w2, 0.0))))
    return out


def _norm_kernel(*refs, add_lanes, has_mod, emit_x, h_dtype, route_dims, final, n_ptiles,
                 n_srows):
    n_add = len(add_lanes)
    weighted = any(l is not None for l in add_lanes)
    it = iter(refs)
    xp_ref, xs_ref = next(it), next(it)
    add_refs = [next(it) for _ in range(n_add)]
    if weighted:
        rw_ref = next(it)
    if n_add:
        gp_ref, gs_ref = next(it), next(it)
    gain_ref = next(it)
    if has_mod:
        scp_ref, scs_ref, shp_ref, shs_ref = next(it), next(it), next(it), next(it)
    if route_dims:
        wr_ref = next(it)
    if emit_x:
        xpo_ref, xso_ref = next(it), next(it)
    if h_dtype is not None:
        h_ref = next(it)
    if route_dims:
        r_ref = next(it)
    if final:
        yp_ref, ys_ref = next(it), next(it)

    def compute(x, adds, rw, gate, sc, sh):
        if adds:
            scaled = [a if l is None else a * rw[:, l:l + 1] for a, l in zip(adds, add_lanes)]
            s = scaled[0]
            for a in scaled[1:]:
                s = s + a
            x = x + gate * s
        h = x * lax.rsqrt(jnp.mean(x * x, axis=-1, keepdims=True) + EPS) * gain_ref[...]
        if has_mod:
            h = h * (1.0 + sc) + sh
        return x, h

    def logits_of(h):
        return jnp.dot(h, wr_ref[...], preferred_element_type=F32,
                       precision=lax.Precision.HIGHEST)

    i = pl.program_id(0)

    @pl.when(i < n_ptiles)
    def _():
        adds = [a[...].astype(F32) for a in add_refs]
        x, h = compute(xp_ref[...], adds, rw_ref[...] if weighted else None,
                       gp_ref[...] if n_add else None,
                       scp_ref[...] if has_mod else None,
                       shp_ref[...] if has_mod else None)
        if emit_x:
            xpo_ref[...] = x
        if h_dtype is not None:
            h_ref[...] = h.astype(h_dtype)
        if route_dims:
            r_ref[...] = _route(logits_of(h), *route_dims)
        if final:
            yp_ref[...] = h

    @pl.when(i == n_ptiles)
    def _():
        adds = [a[0:n_srows].astype(F32) for a in add_refs]
        x, h = compute(xs_ref[...], adds, rw_ref[0:n_srows] if weighted else None,
                       gs_ref[...] if n_add else None,
                       scs_ref[...] if has_mod else None,
                       shs_ref[...] if has_mod else None)
        if emit_x:
            xso_ref[...] = x
        if h_dtype is not None:
            h_ref[0:n_srows] = h.astype(h_dtype)
        if route_dims:
            r_ref[0:n_srows] = _route(logits_of(h), *route_dims)
        if final:
            ys_ref[...] = h


def _norm_call(xp, xs, seq_len, adds, gate, gain, mod, *, emit_x, name, h_dtype=None,
               route=None, route_w=None, final=False):
    n_p, d = xp.shape
    n_s = xs.shape[0]
    te = min(ELEM_TILE_ROWS, seq_len)
    assert seq_len % te == 0 and n_s <= te and n_s % 8 == 0
    tiles_per_seq = seq_len // te
    n_ptiles = n_p // te
    n_tot = n_p + n_s

    def pidx(i):
        return jnp.minimum(i, n_ptiles - 1)

    def modp_spec(layer, chunk):
        return pl.BlockSpec((None, None, 1, d),
                            lambda i: (layer, pidx(i) // tiles_per_seq, 0, chunk))

    def mods_spec(layer, chunk):
        return pl.BlockSpec((None, n_s, d), lambda i: (layer, 0, chunk))

    args = [xp, xs]
    specs = [pl.BlockSpec((te, d), lambda i: (pidx(i), 0)),
             pl.BlockSpec((n_s, d), lambda i: (0, 0))]
    for arr, plane, _ in adds:
        args.append(arr)
        if plane is None:
            specs.append(pl.BlockSpec((te, d), lambda i: (i, 0)))
        else:
            specs.append(pl.BlockSpec((None, te, d), lambda i, plane=plane: (plane, i, 0)))
    if route_w is not None:
        args.append(route_w)
        specs.append(pl.BlockSpec((te, ROUTER_LANES), lambda i: (i, 0)))
    if adds:
        mp, ms, layer, chunk = gate
        args += [mp, ms]
        specs += [modp_spec(layer, chunk), mods_spec(layer, chunk)]
    args.append(gain)
    specs.append(pl.BlockSpec((1, d), lambda i: (0, 0)))
    if mod is not None:
        mp, ms, layer, sc_chunk, sh_chunk = mod
        args += [mp, ms, mp, ms]
        specs += [modp_spec(layer, sc_chunk), mods_spec(layer, sc_chunk),
                  modp_spec(layer, sh_chunk), mods_spec(layer, sh_chunk)]
    route_dims = None
    if route is not None:
        wr, n_groups, per_group = route
        route_dims = (n_groups, per_group)
        args.append(wr)
        specs.append(pl.BlockSpec(wr.shape, lambda i: (0, 0)))

    out_shapes, out_specs = [], []
    if emit_x:
        out_shapes += [jax.ShapeDtypeStruct((n_p, d), F32), jax.ShapeDtypeStruct((n_s, d), F32)]
        out_specs += [pl.BlockSpec((te, d), lambda i: (pidx(i), 0)),
                      pl.BlockSpec((n_s, d), lambda i: (0, 0))]
    if h_dtype is not None:
        out_shapes.append(jax.ShapeDtypeStruct((n_tot, d), h_dtype))
        out_specs.append(pl.BlockSpec((te, d), lambda i: (i, 0)))
    if route is not None:
        out_shapes.append(jax.ShapeDtypeStruct((n_tot, ROUTER_LANES), F32))
        out_specs.append(pl.BlockSpec((te, ROUTER_LANES), lambda i: (i, 0)))
    if final:
        out_shapes += [jax.ShapeDtypeStruct((n_p, d), F32), jax.ShapeDtypeStruct((n_s, d), F32)]
        out_specs += [pl.BlockSpec((te, d), lambda i: (pidx(i), 0)),
                      pl.BlockSpec((n_s, d), lambda i: (0, 0))]

    kern = functools.partial(_norm_kernel, add_lanes=tuple(l for _, _, l in adds),
                             has_mod=mod is not None, emit_x=emit_x, h_dtype=h_dtype,
                             route_dims=route_dims, final=final, n_ptiles=n_ptiles, n_srows=n_s)
    return pl.pallas_call(
        kern, out_shape=out_shapes, grid=(n_ptiles + 1,),
        in_specs=specs, out_specs=out_specs, name=name, compiler_params=_cparams(1),
    )(*args)


def _mm_kernel(x_ref, w_ref, o_ref):
    o_ref[...] = jnp.dot(x_ref[...], w_ref[...].astype(BF16),
                         preferred_element_type=F32).astype(o_ref.dtype)


def _matmul(x, w, layer, tn, name, out_dtype=BF16):
    m, k = x.shape
    n = w.shape[2]
    tm = _matmul_rows(m)
    assert n % tn == 0
    return pl.pallas_call(
        _mm_kernel,
        out_shape=jax.ShapeDtypeStruct((m, n), out_dtype),
        grid=(m // tm, n // tn),
        in_specs=[pl.BlockSpec((tm, k), lambda i, j: (i, 0)),
                  pl.BlockSpec((None, k, tn), lambda i, j: (layer, 0, j))],
        out_specs=pl.BlockSpec((tm, tn), lambda i, j: (i, j)),
        name=name,
        compiler_params=_cparams(2),
    )(x, w)


def _merge_kernel(a_ref, r_ref, g0_ref, g1_ref, wa_ref, wb_ref, o_ref):
    ya = jnp.dot(a_ref[...], wa_ref[...].astype(BF16), preferred_element_type=F32)
    yb = jnp.dot(r_ref[...], wb_ref[...].astype(BF16), preferred_element_type=F32)
    g0 = jax.nn.sigmoid(g0_ref[...].astype(F32))
    g1 = jax.nn.sigmoid(g1_ref[...].astype(F32))
    o_ref[...] = (g0 * ya + g1 * yb).astype(o_ref.dtype)


def _merge(a, rn, proj, gate_col0, w_a, w_b, layer, tn):
    m, ka = a.shape
    kb = rn.shape[1]
    d = w_a.shape[2]
    tm = _matmul_rows(m)
    assert d % tn == 0 and gate_col0 % tn == 0
    c0 = gate_col0 // tn
    c1 = (gate_col0 + d) // tn
    return pl.pallas_call(
        _merge_kernel,
        out_shape=jax.ShapeDtypeStruct((m, d), BF16),
        grid=(m // tm, d // tn),
        in_specs=[pl.BlockSpec((tm, ka), lambda i, j: (i, 0)),
                  pl.BlockSpec((tm, kb), lambda i, j: (i, 0)),
                  pl.BlockSpec((tm, tn), lambda i, j: (i, c0 + j)),
                  pl.BlockSpec((tm, tn), lambda i, j: (i, c1 + j)),
                  pl.BlockSpec((None, ka, tn), lambda i, j: (layer, 0, j)),
                  pl.BlockSpec((None, kb, tn), lambda i, j: (layer, 0, j))],
        out_specs=pl.BlockSpec((tm, tn), lambda i, j: (i, j)),
        name="branch_merge",
        compiler_params=_cparams(2),
    )(a, rn, proj, proj, w_a, w_b)


def _rope_tables(pos, rot_dim, head_dim, base):
    half = rot_dim // 2
    inv = 1.0 / (base ** (jnp.arange(half, dtype=F32) * (2.0 / rot_dim)))
    ang = pos.astype(F32)[:, None] * inv[None, :]
    cos, sin = jnp.cos(ang), jnp.sin(ang)
    t = pos.shape[0]
    rest = head_dim - rot_dim
    cos_h = jnp.concatenate([cos, cos, jnp.ones((t, rest), F32)], axis=1)
    sin_lo = jnp.concatenate([jnp.zeros((t, half), F32), sin, jnp.zeros((t, rest), F32)], axis=1)
    sin_hi = jnp.concatenate([-sin, jnp.zeros((t, half + rest), F32)], axis=1)
    reps = LANES // head_dim
    tile = lambda a: jnp.tile(a, (1, reps))
    return tile(cos_h), tile(sin_lo), tile(sin_hi)


def _rope_lanes(x, cos, sin_lo, sin_hi, half):
    return (x * cos + pltpu.roll(x, half, axis=1) * sin_lo
            + pltpu.roll(x, LANES - half, axis=1) * sin_hi)


def _rope_wide(x, cos, sin_lo, sin_hi, half):
    cols = [_rope_lanes(x[:, c:c + LANES], cos, sin_lo, sin_hi, half)
            for c in range(0, x.shape[1], LANES)]
    return cols[0] if len(cols) == 1 else jnp.concatenate(cols, axis=1)


def _dot_nt(a, b):
    return lax.dot_general(a, b, (((1,), (1,)), ((), ())), preferred_element_type=F32)


def _dot_tn(a, b):
    return lax.dot_general(a, b, (((0,), (0,)), ((), ())), preferred_element_type=F32)


def _sink_softmax_pv(score_parts, value_parts, sink_col):
    m = sink_col
    for s in score_parts:
        m = jnp.maximum(m, jnp.max(s, axis=1, keepdims=True))
    den = jnp.exp(sink_col - m)
    acc = None
    for s, v in zip(score_parts, value_parts):
        p = jnp.exp(s - m)
        den = den + jnp.sum(p, axis=1, keepdims=True)
        pv = jnp.dot(p.astype(BF16), v, preferred_element_type=F32)
        acc = pv if acc is None else acc + pv
    return acc * (1.0 / den)


def _swa_prompt_kernel(sink_ref, q_ref, kc_ref, kp_ref, vc_ref, vp_ref,
                       cc_ref, lc_ref, hc_ref, cp_ref, lp_ref, hp_ref, bias_ref,
                       a_ref, nk_ref, nv_ref, *, layer, n_heads, n_kv, hd, half):
    w = q_ref.shape[0]
    grp = n_heads // n_kv
    tabs_c = (cc_ref[...], lc_ref[...], hc_ref[...])
    tabs_p = (cp_ref[...], lp_ref[...], hp_ref[...])
    q = (_rope_wide(q_ref[...].astype(F32), *tabs_c, half) * (hd ** -0.5)).astype(BF16)
    kc_f = _rope_wide(kc_ref[...].astype(F32), *tabs_c, half)
    kc = kc_f.astype(BF16)
    kp = _rope_wide(kp_ref[...].astype(F32), *tabs_p, half).astype(BF16)
    vc, vp = vc_ref[...], vp_ref[...]
    bias = bias_ref[...]
    bias_g = jnp.concatenate([bias] * grp, axis=0)
    outs = []
    for g in range(n_kv):
        ks = slice(g * hd, (g + 1) * hd)
        qg = jnp.concatenate([q[:, (g * grp + j) * hd:(g * grp + j + 1) * hd]
                              for j in range(grp)], axis=0)
        kk = jnp.concatenate([kp[:, ks], kc[:, ks]], axis=0)
        vv = jnp.concatenate([vp[:, ks], vc[:, ks]], axis=0)
        sink = jnp.concatenate([jnp.full((w, 1), sink_ref[layer, g * grp + j], F32)
                                for j in range(grp)], axis=0)
        s = _dot_nt(qg, kk) + bias_g
        o = _sink_softmax_pv([s], [vv], sink)
        outs += [o[j * w:(j + 1) * w] for j in range(grp)]
    a_ref[...] = jnp.concatenate(outs, axis=1).astype(a_ref.dtype)

    @pl.when(pl.program_id(1) == pl.num_programs(1) - 1)
    def _():
        nk_ref[...] = kc_f
        nv_ref[...] = vc.astype(F32)


def _swa_prompt(proj, sinks, layer, bp, seq_len, n_rows, n_heads, n_kv, hd, tabs, bias):
    w = WINDOW
    nb = seq_len // w
    qw, kvw = n_heads * hd, n_kv * hd
    assert qw % kvw == 0
    kcol, vcol = qw // kvw, qw // kvw + 1
    cur = lambda b, n: b * nb + n
    prev = lambda b, n: b * nb + jnp.maximum(n - 1, 0)
    tab_cur = pl.BlockSpec((w, LANES), lambda b, n: (n, 0))
    tab_prev = pl.BlockSpec((w, LANES), lambda b, n: (jnp.maximum(n - 1, 0), 0))
    kern = functools.partial(_swa_prompt_kernel, layer=layer, n_heads=n_heads, n_kv=n_kv,
                             hd=hd, half=hd // 8)
    return pl.pallas_call(
        kern,
        out_shape=[jax.ShapeDtypeStruct((n_rows, qw), BF16),
                   jax.ShapeDtypeStruct((bp, w, kvw), F32),
                   jax.ShapeDtypeStruct((bp, w, kvw), F32)],
        grid=(bp, nb),
        in_specs=[pl.BlockSpec(memory_space=pltpu.SMEM),
                  pl.BlockSpec((w, qw), lambda b, n: (cur(b, n), 0)),
                  pl.BlockSpec((w, kvw), lambda b, n: (cur(b, n), kcol)),
                  pl.BlockSpec((w, kvw), lambda b, n: (prev(b, n), kcol)),
                  pl.BlockSpec((w, kvw), lambda b, n: (cur(b, n), vcol)),
                  pl.BlockSpec((w, kvw), lambda b, n: (prev(b, n), vcol)),
                  tab_cur, tab_cur, tab_cur, tab_prev, tab_prev, tab_prev,
                  pl.BlockSpec((None, w, 2 * w), lambda b, n: (jnp.minimum(n, 1), 0, 0))],
        out_specs=[pl.BlockSpec((w, qw), lambda b, n: (cur(b, n), 0)),
                   pl.BlockSpec((None, w, kvw), lambda b, n: (b, 0, 0)),
                   pl.BlockSpec((None, w, kvw), lambda b, n: (b, 0, 0))],
        name="swa_prompt",
        compiler_params=_cparams(2),
    )(sinks, proj, proj, proj, proj, proj, *tabs, *tabs, bias)


def _swa_sample_kernel(sink_ref, a_in_ref, q_ref, k_ref, v_ref, kb_ref, vb_ref,
                       c_ref, l_ref, h_ref, bias_c_ref, bias_n_ref,
                       a_ref, nk_ref, nv_ref, *, layer, n_heads, n_kv, hd, half):
    del a_in_ref
    rows = q_ref.shape[0]
    grp = n_heads // n_kv
    tabs = (c_ref[...], l_ref[...], h_ref[...])
    q = (_rope_wide(q_ref[...].astype(F32), *tabs, half) * (hd ** -0.5)).astype(BF16)
    kn_f = _rope_wide(k_ref[...].astype(F32), *tabs, half)
    kn = kn_f.astype(BF16)
    vn = v_ref[...]
    kb = kb_ref[...].astype(BF16)
    vb = vb_ref[...].astype(BF16)
    bias_c = bias_c_ref[...]
    bias_n = bias_n_ref[...]
    outs = []
    for g in range(n_kv):
        ks = slice(g * hd, (g + 1) * hd)
        qg = jnp.concatenate([q[:, (g * grp + j) * hd:(g * grp + j + 1) * hd]
                              for j in range(grp)], axis=0)
        sink = jnp.concatenate([jnp.full((rows, 1), sink_ref[layer, g * grp + j], F32)
                                for j in range(grp)], axis=0)
        s_c = _dot_nt(qg, kb[:, ks]) + bias_c
        s_n = _dot_nt(qg, kn[:, ks]) + bias_n
        o = _sink_softmax_pv([s_c, s_n], [vb[:, ks], vn[:, ks]], sink)
        outs += [o[j * rows:(j + 1) * rows] for j in range(grp)]
    a_ref[...] = jnp.concatenate(outs, axis=1).astype(a_ref.dtype)
    nk_ref[...] = kn_f
    nv_ref[...] = vn.astype(F32)


def _swa_sample(proj, a, sinks, cache_k, cache_v, layer, row0, bs, ts, n_heads, n_kv, hd,
                tabs, bias_c, bias_n):
    sg = SAMPLE_SEQS_PER_STEP
    rows = sg * ts
    wb = cache_k.shape[1] // bs
    qw, kvw = n_heads * hd, n_kv * hd
    assert bs % sg == 0 and row0 % rows == 0 and rows % BF16_SUBLANES == 0
    r0 = row0 // rows
    kcol, vcol = qw // kvw, qw // kvw + 1
    full = lambda shape: pl.BlockSpec(shape, lambda s: (0,) * len(shape))
    kern = functools.partial(_swa_sample_kernel, layer=layer, n_heads=n_heads, n_kv=n_kv,
                             hd=hd, half=hd // 8)
    return pl.pallas_call(
        kern,
        out_shape=[jax.ShapeDtypeStruct(a.shape, a.dtype),
                   jax.ShapeDtypeStruct((bs * ts, kvw), F32),
                   jax.ShapeDtypeStruct((bs * ts, kvw), F32)],
        grid=(bs // sg,),
        in_specs=[pl.BlockSpec(memory_space=pltpu.SMEM),
                  pl.BlockSpec(memory_space=pl.ANY),
                  pl.BlockSpec((rows, qw), lambda s: (r0 + s, 0)),
                  pl.BlockSpec((rows, kvw), lambda s: (r0 + s, kcol)),
                  pl.BlockSpec((rows, kvw), lambda s: (r0 + s, vcol)),
                  pl.BlockSpec((None, sg * wb, kvw), lambda s: (layer, s, 0)),
                  pl.BlockSpec((None, sg * wb, kvw), lambda s: (layer, s, 0)),
                  full((rows, LANES)), full((rows, LANES)), full((rows, LANES)),
                  full(bias_c.shape), full(bias_n.shape)],
        out_specs=[pl.BlockSpec((rows, qw), lambda s: (r0 + s, 0)),
                   pl.BlockSpec((rows, kvw), lambda s: (s, 0)),
                   pl.BlockSpec((rows, kvw), lambda s: (s, 0))],
        input_output_aliases={1: 0},
        name="swa_sample",
        compiler_params=_cparams(1),
    )(sinks, a, proj, proj, proj, cache_k, cache_v, *tabs, bias_c, bias_n)


def _ret_head(q_raw, k_raw, v, g_raw, cos, sin, scale):
    half = LANES // 2
    q = q_raw * cos + pltpu.roll(q_raw, half, axis=1) * sin
    k = (k_raw * cos + pltpu.roll(k_raw, half, axis=1) * sin) * scale
    return q, k


def _ret_finish(o, g_raw):
    n = o * lax.rsqrt(jnp.mean(o * o, axis=-1, keepdims=True) + EPS)
    return n * _silu(g_raw)


def _ret_prompt_kernel(q_ref, k_ref, v_ref, g_ref, cos_ref, sin_ref,
                       intra_ref, qd_ref, kd_ref, cd_ref,
                       rn_ref, st_ref, s_ref, *, heads, scale):
    c = pl.program_id(2)

    @pl.when(c == 0)
    def _():
        s_ref[...] = jnp.zeros_like(s_ref)

    cos, sin = cos_ref[...], sin_ref[...]
    for hh in range(heads):
        sl = slice(hh * LANES, (hh + 1) * LANES)
        q, k = _ret_head(q_ref[:, sl].astype(F32), k_ref[:, sl].astype(F32), None, None,
                         cos, sin, scale)
        v = v_ref[:, sl]
        att = _dot_nt(q.astype(BF16), k.astype(BF16)) * intra_ref[hh]
        s0 = s_ref[hh]
        o = (jnp.dot(att.astype(BF16), v, preferred_element_type=F32)
             + jnp.dot((q * qd_ref[hh]).astype(BF16), s0.astype(BF16),
                       preferred_element_type=F32))
        s_ref[hh] = s0 * cd_ref[hh] + _dot_tn((k * kd_ref[hh]).astype(BF16), v)
        rn_ref[:, sl] = _ret_finish(o, g_ref[:, sl].astype(F32)).astype(rn_ref.dtype)

    @pl.when(c == pl.num_programs(2) - 1)
    def _():
        st_ref[...] = s_ref[...]


def _ret_prompt(proj, layer, bp, seq_len, n_rows, col0, n_heads, tabs, dec):
    del layer
    ch = RET_CHUNK
    nc = seq_len // ch
    hw = n_heads * LANES
    hg = 4 if n_heads % 4 == 0 else n_heads
    bw = hg * LANES
    nhalf = n_heads // hg
    assert col0 % bw == 0 and hw % bw == 0
    cb = lambda k: (col0 + k * hw) // bw
    row = lambda b, hf, c: b * nc + c
    inspec = lambda k: pl.BlockSpec((ch, bw), lambda b, hf, c: (row(b, hf, c), cb(k) + hf))
    tab = pl.BlockSpec((ch, LANES), lambda b, hf, c: (c, 0))
    dspec = pl.BlockSpec((hg, ch, LANES), lambda b, hf, c: (hf, 0, 0))
    kern = functools.partial(_ret_prompt_kernel, heads=hg, scale=LANES ** -0.5)
    return pl.pallas_call(
        kern,
        out_shape=[jax.ShapeDtypeStruct((n_rows, hw), BF16),
                   jax.ShapeDtypeStruct((bp, n_heads, LANES, LANES), F32)],
        grid=(bp, nhalf, nc),
        in_specs=[inspec(0), inspec(1), inspec(2), inspec(3), tab, tab,
                  dspec, dspec, dspec, dspec],
        out_specs=[pl.BlockSpec((ch, bw), lambda b, hf, c: (row(b, hf, c), hf)),
                   pl.BlockSpec((None, hg, LANES, LANES), lambda b, hf, c: (b, hf, 0, 0))],
        scratch_shapes=[pltpu.VMEM((hg, LANES, LANES), F32)],
        name="ret_prompt",
        compiler_params=_cparams(3),
    )(proj, proj, proj, proj, *tabs, *dec)


def _ret_sample_kernel(rn_in_ref, q_ref, k_ref, v_ref, g_ref, s_in_ref, cos_ref, sin_ref,
                       intra_ref, qd_ref, kd_ref, cd_ref, sel_ref,
                       rn_ref, s_out_ref, *, heads, seqs, scale):
    del rn_in_ref
    cos, sin = cos_ref[...], sin_ref[...]
    for hh in range(heads):
        sl = slice(hh * LANES, (hh + 1) * LANES)
        q, k = _ret_head(q_ref[:, sl].astype(F32), k_ref[:, sl].astype(F32), None, None,
                         cos, sin, scale)
        v = v_ref[:, sl]
        att = _dot_nt(q.astype(BF16), k.astype(BF16)) * intra_ref[hh]
        o = jnp.dot(att.astype(BF16), v, preferred_element_type=F32)
        qd = (q * qd_ref[hh]).astype(BF16)
        kd = k * kd_ref[hh]
        for s in range(seqs):
            sel = sel_ref[s]
            s0 = s_in_ref[s, hh]
            o = o + sel * jnp.dot(qd, s0.astype(BF16), preferred_element_type=F32)
            s_out_ref[s, hh] = s0 * cd_ref[hh] + _dot_tn((kd * sel).astype(BF16), v)
        rn_ref[:, sl] = _ret_finish(o, g_ref[:, sl].astype(F32)).astype(rn_ref.dtype)


def _ret_sample(proj, rn, state, layer, row0, bs, ts, col0, n_heads, tabs, dec, sel):
    sg = SAMPLE_SEQS_PER_STEP
    rows = sg * ts
    hw = n_heads * LANES
    hg = 4 if n_heads % 4 == 0 else n_heads
    bw = hg * LANES
    nhalf = n_heads // hg
    assert bs % sg == 0 and row0 % rows == 0 and col0 % bw == 0
    r0 = row0 // rows
    cb = lambda k: (col0 + k * hw) // bw
    inspec = lambda k: pl.BlockSpec((rows, bw), lambda s, hf: (r0 + s, cb(k) + hf))
    tab = pl.BlockSpec((rows, LANES), lambda s, hf: (0, 0))
    dspec = lambda a: pl.BlockSpec((hg,) + a.shape[1:], lambda s, hf: (hf, 0, 0))
    st_spec = pl.BlockSpec((None, sg, hg, LANES, LANES), lambda s, hf: (layer, s, hf, 0, 0))
    kern = functools.partial(_ret_sample_kernel, heads=hg, seqs=sg, scale=LANES ** -0.5)
    return pl.pallas_call(
        kern,
        out_shape=[jax.ShapeDtypeStruct(rn.shape, rn.dtype),
                   jax.ShapeDtypeStruct(state.shape[1:], F32)],
        grid=(bs // sg, nhalf),
        in_specs=[pl.BlockSpec(memory_space=pl.ANY),
                  inspec(0), inspec(1), inspec(2), inspec(3), st_spec, tab, tab,
                  dspec(dec[0]), dspec(dec[1]), dspec(dec[2]), dspec(dec[3]),
                  pl.BlockSpec(sel.shape, lambda s, hf: (0, 0, 0))],
        out_specs=[pl.BlockSpec((rows, bw), lambda s, hf: (r0 + s, hf)),
                   pl.BlockSpec((sg, hg, LANES, LANES), lambda s, hf: (s, hf, 0, 0))],
        input_output_aliases={0: 0},
        name="ret_sample",
        compiler_params=_cparams(2),
    )(rn, proj, proj, proj, proj, state, *tabs, *dec, sel)


def _ret_decay_tables(n_heads, chunk, rows):
    log_g = jnp.log(1.0 - jnp.exp2(-5.0 - jnp.arange(n_heads, dtype=F32)))
    i = jnp.arange(chunk, dtype=F32)
    diff = i[:, None] - i[None, :]
    intra = jnp.where(diff >= 0, jnp.exp(jnp.maximum(diff, 0.0) * log_g[:, None, None]), 0.0)
    q_dec = jnp.exp((i + 1.0) * log_g[:, None])
    k_dec = jnp.exp((chunk - 1.0 - i) * log_g[:, None])
    c_dec = jnp.exp(chunk * log_g)
    reps = rows // chunk
    same = jnp.kron(jnp.eye(reps, dtype=F32), jnp.ones((chunk, chunk), F32))
    intra_t = jnp.tile(intra, (1, reps, reps)) * same[None]
    qd_t = jnp.broadcast_to(jnp.tile(q_dec, (1, reps))[:, :, None], (n_heads, rows, LANES))
    kd_t = jnp.broadcast_to(jnp.tile(k_dec, (1, reps))[:, :, None], (n_heads, rows, LANES))
    cd_t = jnp.broadcast_to(c_dec[:, None, None], (n_heads, LANES, LANES))
    return intra_t, qd_t, kd_t, cd_t


def _ret_rope_tables(pos):
    half = LANES // 2
    inv = 1.0 / (RET_ROT_BASE ** (jnp.arange(half, dtype=F32) * (2.0 / LANES)))
    ang = pos.astype(F32)[:, None] * inv[None, :]
    cos, sin = jnp.cos(ang), jnp.sin(ang)
    return jnp.concatenate([cos, cos], axis=1), jnp.concatenate([-sin, sin], axis=1)


def _moe_plan(route_out, n_tok, n_exp, tm, plane_rows):
    n_asg = 2 * n_tok
    n_tiles = n_asg // tm
    e_flat = route_out[:, 0:2].astype(I32).reshape(n_asg)
    order = jnp.argsort(e_flat, stable=True).astype(I32)
    tok = order >> 1
    dst = (order & 1) * plane_rows + tok
    cnt = jnp.sum((e_flat[:, None] == jnp.arange(n_exp, dtype=I32)[None, :]).astype(I32), axis=0)
    cend = jnp.cumsum(cnt)
    cstart = cend - cnt
    first_tile = cstart // tm
    n_e = jnp.where(cnt > 0, (cend - 1) // tm - first_tile + 1, 0)
    pend = jnp.cumsum(n_e)
    pstart = pend - n_e
    n_steps = pend[-1]
    s = jnp.minimum(jnp.arange(n_tiles + n_exp, dtype=I32), n_steps - 1)
    se = jnp.sum((pend[None, :] <= s[:, None]).astype(I32), axis=1)
    st = jnp.take(first_tile, se) + s - jnp.take(pstart, se)
    lo = jnp.clip(jnp.take(cstart, se) - st * tm, 0, tm)
    hi = jnp.clip(jnp.take(cend, se) - st * tm, 0, tm)
    as_i32 = lambda a: a.astype(I32)
    return (as_i32(se), as_i32(st), as_i32(lo), as_i32(hi), as_i32(n_steps).reshape(1),
            as_i32(tok), as_i32(dst))


def _moe_kernel(se_ref, st_ref, lo_ref, hi_ref, ns_ref, tok_ref, dst_ref,
                h_hbm, wg_ref, wu_ref, wd_ref,
                y_hbm,
                gbuf, obuf, wg_bf, wu_bf, wd_bf, gsem, ssem, *, n_tiles):
    s = pl.program_id(0)
    n_steps = ns_ref[0]
    tm, d = obuf.shape[1], obuf.shape[2]
    ff = wg_bf.shape[1]

    def gather_row(tile, slot, r):
        pltpu.make_async_copy(h_hbm.at[pl.ds(tok_ref[tile * tm + r], 1)],
                              gbuf.at[slot, pl.ds(r, 1)], gsem.at[slot]).start()

    def gather_wait(slot):
        pltpu.make_async_copy(h_hbm.at[pl.ds(0, tm)], gbuf.at[slot], gsem.at[slot]).wait()

    def scatter_row(tile, slot, r):
        pltpu.make_async_copy(obuf.at[slot, pl.ds(r, 1)],
                              y_hbm.at[pl.ds(dst_ref[tile * tm + r], 1)], ssem.at[slot]).start()

    def scatter_wait(slot):
        pltpu.make_async_copy(obuf.at[slot], y_hbm.at[pl.ds(0, tm)], ssem.at[slot]).wait()

    def experts(slot, lo, hi, accumulate, row_dmas):
        n_chunks = MOE_OUT_CHUNKS
        cw = d // n_chunks
        n_gaps = 3 + n_chunks
        per_gap = -(-len(row_dmas) // n_gaps)

        def issue(k):
            for thunk in row_dmas[k * per_gap:(k + 1) * per_gap]:
                thunk()

        x = gbuf[slot].astype(BF16)
        issue(0)
        g = jnp.dot(x, wg_bf[...], preferred_element_type=F32)
        issue(1)
        u = jnp.dot(x, wu_bf[...], preferred_element_type=F32)
        issue(2)
        row = lax.broadcasted_iota(I32, (tm, ff), 0)
        owned = jnp.logical_and(row >= lo, row < hi)
        hid = jnp.where(owned, _silu(g) * u, 0.0).astype(BF16)
        for c in range(n_chunks):
            cols = slice(c * cw, (c + 1) * cw)
            y = jnp.dot(hid, wd_bf[:, cols], preferred_element_type=F32)
            if accumulate:
                obuf[slot, :, cols] += y
            else:
                obuf[slot, :, cols] = y
            issue(3 + c)

    @pl.when(s < n_steps)
    def _():
        tile = st_ref[s]
        slot = lax.rem(tile, 2)
        prev = jnp.maximum(s - 1, 0)
        first = jnp.logical_or(s == 0, tile != st_ref[prev])
        expert_changed = jnp.logical_or(s == 0, se_ref[s] != se_ref[prev])
        lo, hi = lo_ref[s], hi_ref[s]
        nxt = jnp.minimum(tile + 1, n_tiles - 1)
        gathers = [functools.partial(gather_row, nxt, 1 - slot, r) for r in range(tm)]
        scatters = [functools.partial(scatter_row, tile - 1, 1 - slot, r) for r in range(tm)]
        both = [t for pair in zip(gathers, scatters) for t in pair]

        @pl.when(s == 0)
        def _():
            for r in range(tm):
                gather_row(0, 0, r)

        @pl.when(first)
        def _():
            gather_wait(slot)

        @pl.when(jnp.logical_and(first, tile >= 2))
        def _():
            scatter_wait(slot)

        @pl.when(expert_changed)
        def _():
            wg_bf[...] = wg_ref[...].astype(BF16)
            wu_bf[...] = wu_ref[...].astype(BF16)
            wd_bf[...] = wd_ref[...].astype(BF16)

        @pl.when(jnp.logical_and(first, tile == 0))
        def _():
            experts(slot, lo, hi, False, gathers)

        @pl.when(jnp.logical_and(first, tile > 0))
        def _():
            experts(slot, lo, hi, False, both)

        @pl.when(jnp.logical_not(first))
        def _():
            experts(slot, lo, hi, True, [])

        @pl.when(s == n_steps - 1)
        def _():
            last_slot = (n_tiles - 1) % 2
            for r in range(tm):
                scatter_row(n_tiles - 1, last_slot, r)
            gather_wait(1 - last_slot)
            for sl in range(min(n_tiles, 2)):
                scatter_wait(sl)


def _moe(h, route_out, w_g, w_u, w_d, layer, plane_rows):
    n_tok, d = h.shape
    n_exp, _, ff = w_g.shape[1:]
    n_asg = 2 * n_tok
    tm = MOE_TILE_ROWS
    while n_asg % tm:
        tm //= 2
    assert tm >= 8 and ff % LANES == 0 and plane_rows >= n_tok
    n_tiles = n_asg // tm
    plan = _moe_plan(route_out, n_tok, n_exp, tm, plane_rows)
    w_in_map = lambda s, se, *_: (layer, se[s], 0, 0)
    grid_spec = pltpu.PrefetchScalarGridSpec(
        num_scalar_prefetch=len(plan),
        grid=(n_tiles + n_exp - 1,),
        in_specs=[pl.BlockSpec(memory_space=pl.ANY),
                  pl.BlockSpec((None, None, d, ff), w_in_map),
                  pl.BlockSpec((None, None, d, ff), w_in_map),
                  pl.BlockSpec((None, None, ff, d), w_in_map)],
        out_specs=pl.BlockSpec(memory_space=pl.ANY),
        scratch_shapes=[pltpu.VMEM((2, tm, d), F32),
                        pltpu.VMEM((2, tm, d), F32),
                        pltpu.VMEM((d, ff), BF16),
                        pltpu.VMEM((d, ff), BF16),
                        pltpu.VMEM((ff, d), BF16),
                        pltpu.SemaphoreType.DMA((2,)),
                        pltpu.SemaphoreType.DMA((2,))],
    )
    y = pl.pallas_call(
        functools.partial(_moe_kernel, n_tiles=n_tiles),
        out_shape=jax.ShapeDtypeStruct((2 * plane_rows, d), F32),
        grid_spec=grid_spec,
        name="moe_experts",
        compiler_params=pltpu.CompilerParams(dimension_semantics=("arbitrary",),
                                             vmem_limit_bytes=VMEM_LIMIT_BYTES,
                                             has_side_effects=True),
    )(*plan, h, w_g, w_u, w_d)
    return y.reshape(2, plane_rows, d)


def _swa_prompt_bias(w):
    r = jnp.arange(w)[:, None]
    c = jnp.arange(2 * w)[None, :]
    band = (c <= r + w) & (c > r)
    first = band & (c >= w)
    return jnp.where(jnp.stack([first, band]), 0.0, -jnp.inf).astype(F32)


def _swa_sample_bias(sg, ts, wb, grp):
    rho = jnp.arange(sg * ts)
    seq_q, t = rho // ts, rho % ts
    cc = jnp.arange(sg * wb)
    ok_c = (cc[None, :] // wb == seq_q[:, None]) & (cc[None, :] % wb - wb > t[:, None] - WINDOW)
    cn = jnp.arange(sg * ts)
    ok_n = (cn[None, :] // ts == seq_q[:, None]) & (cn[None, :] % ts <= t[:, None])
    to_bias = lambda ok: jnp.tile(jnp.where(ok, 0.0, -jnp.inf).astype(F32), (grp, 1))
    return to_bias(ok_c), to_bias(ok_n)


def kernel(x_prompt, x_sample, cache_win_k, cache_win_v, state_ret, c_prompt, c_sample,
           w_ada, b_ada, g_mix, g_ffn, w_in, attn_sinks, w_branch_attn, w_branch_ret, w_out,
           w_route_group, w_route_expert, w_exp_gate, w_exp_up, w_exp_down, g_final):
    bp, seq_len, d = x_prompt.shape
    bs, ts, _ = x_sample.shape
    depth = w_in.shape[0]
    n_heads = attn_sinks.shape[1]
    _, _, wb, n_kv, hd = cache_win_k.shape
    ret_heads, ret_hd = state_ret.shape[2], state_ret.shape[3]
    n_groups = w_route_group.shape[2]
    n_exp = w_route_expert.shape[2]
    per_group = n_exp // n_groups
    qw, kvw, rw = n_heads * hd, n_kv * hd, ret_heads * ret_hd
    assert ret_hd == LANES and LANES % hd == 0 and wb == WINDOW
    assert seq_len % WINDOW == 0 and seq_len % RET_CHUNK == 0 and ts < RET_CHUNK
    n_p, n_s = bp * seq_len, bs * ts
    n_tot = n_p + n_s
    ret_col0 = qw + 2 * kvw
    gate_col0 = ret_col0 + 4 * rw
    proj_tn = 512

    r_pad = -(-(bp + bs) // BF16_SUBLANES) * BF16_SUBLANES
    c_all = jnp.concatenate([c_prompt, c_sample, jnp.zeros((r_pad - bp - bs, d), F32)], axis=0)
    mod = _ada(c_all, w_ada, b_ada)
    mod_p = mod[:, :bp].reshape(depth, bp, 1, 6 * d)
    mod_s = jnp.repeat(mod[:, bp:bp + bs], ts, axis=1)

    pos_p = jnp.arange(seq_len, dtype=I32)
    pos_s = PAST_LEN + jnp.arange(ts, dtype=I32)
    sg = SAMPLE_SEQS_PER_STEP
    swa_tabs_p = _rope_tables(pos_p, hd // 4, hd, ROPE_THETA)
    swa_tabs_s = tuple(jnp.tile(t, (sg, 1)) for t in _rope_tables(pos_s, hd // 4, hd, ROPE_THETA))
    ret_tabs_p = _ret_rope_tables(pos_p)
    ret_tabs_s = tuple(jnp.tile(t, (sg, 1)) for t in _ret_rope_tables(pos_s))
    dec_p = _ret_decay_tables(ret_heads, RET_CHUNK, RET_CHUNK)
    dec_s = _ret_decay_tables(ret_heads, ts, sg * ts)
    sel_s = jnp.broadcast_to(
        (jnp.arange(sg * ts)[None, :] // ts == jnp.arange(sg)[:, None]).astype(F32)[:, :, None],
        (sg, sg * ts, LANES))
    bias_p = _swa_prompt_bias(WINDOW)
    bias_sc, bias_sn = _swa_sample_bias(sg, ts, wb, n_heads // n_kv)

    w_route = jnp.concatenate(
        [w_route_group, w_route_expert,
         jnp.zeros((depth, d, ROUTER_LANES - n_groups - n_exp), F32)], axis=2)
    cache_k = cache_win_k.reshape(depth, bs * wb, kvw)
    cache_v = cache_win_v.reshape(depth, bs * wb, kvw)

    xp = x_prompt.reshape(n_p, d)
    xs = x_sample.reshape(n_s, d)
    moe_out = route_out = None
    outs = {k: [] for k in ("pk", "pv", "ps", "sk", "sv", "ss")}
    for l in range(depth):
        adds = [] if moe_out is None else [(moe_out, 0, 2), (moe_out, 1, 3)]
        gate = None if moe_out is None else (mod_p, mod_s, l - 1, 5)
        res = _norm_call(xp, xs, seq_len, adds, gate, g_mix[l].reshape(1, d),
                         (mod_p, mod_s, l, 1, 0), emit_x=moe_out is not None, h_dtype=BF16,
                         route_w=route_out, name=f"norm_mix_{l}")
        if moe_out is not None:
            xp, xs, h = res
        else:
            (h,) = res
        proj = _matmul(h, w_in, l, proj_tn, "in_proj")

        a, pk, pv = _swa_prompt(proj, attn_sinks, l, bp, seq_len, n_tot, n_heads, n_kv, hd,
                                swa_tabs_p, bias_p)
        a, k_new, v_new = _swa_sample(proj, a, attn_sinks, cache_k, cache_v, l, n_p, bs, ts,
                                      n_heads, n_kv, hd, swa_tabs_s, bias_sc, bias_sn)
        rn, st_p = _ret_prompt(proj, l, bp, seq_len, n_tot, ret_col0, ret_heads, ret_tabs_p, dec_p)
        rn, st_s = _ret_sample(proj, rn, state_ret, l, n_p, bs, ts, ret_col0, ret_heads,
                               ret_tabs_s, dec_s, sel_s)
        merged = _merge(a, rn, proj, gate_col0, w_branch_attn, w_branch_ret, l, min(proj_tn, d))
        mix = _matmul(merged, w_out, l, min(proj_tn, d), "out_proj")

        outs["pk"].append(pk.reshape(bp, wb, n_kv, hd))
        outs["pv"].append(pv.reshape(bp, wb, n_kv, hd))
        outs["ps"].append(st_p)
        outs["sk"].append(jnp.concatenate([cache_win_k[l][:, ts:], k_new.reshape(bs, ts, n_kv, hd)], axis=1))
        outs["sv"].append(jnp.concatenate([cache_win_v[l][:, ts:], v_new.reshape(bs, ts, n_kv, hd)], axis=1))
        outs["ss"].append(st_s)

        xp, xs, h2, route_out = _norm_call(
            xp, xs, seq_len, [(mix, None, None)], (mod_p, mod_s, l, 2), g_ffn[l].reshape(1, d),
            (mod_p, mod_s, l, 4, 3), emit_x=True, h_dtype=F32,
            route=(w_route[l], n_groups, per_group), name=f"norm_ffn_route_{l}")
        moe_out = _moe(h2, route_out, w_exp_gate, w_exp_up, w_exp_down, l, n_tot)

    yp, ys = _norm_call(xp, xs, seq_len, [(moe_out, 0, 2), (moe_out, 1, 3)],
                        (mod_p, mod_s, depth - 1, 5), g_final.reshape(1, d), None,
                        emit_x=False, final=True, route_w=route_out, name="norm_final")
    stack = lambda k: jnp.stack(outs[k])
    return (yp.reshape(bp, seq_len, d), ys.reshape(bs, ts, d),
            stack("pk"), stack("pv"), stack("ps"), stack("sk"), stack("sv"), stack("ss"))
```

```python
import functools

import jax
import jax.numpy as jnp
from jax import lax
from jax.experimental import pallas as pl
from jax.experimental.pallas import tpu as pltpu

F32 = jnp.float32
BF16 = jnp.bfloat16
I32 = jnp.int32

PAST_LEN = 16384
WINDOW = 128
ROPE_THETA = 500000.0
RET_CHUNK = 128
RET_ROT_BASE = 10000.0
EPS = 1e-6

LANES = 128
BF16_SUBLANES = 16
VMEM_LIMIT_BYTES = 56 * 1024 * 1024
ELEM_TILE_ROWS = 512
MATMUL_MAX_ROWS = 2048
SAMPLE_SEQS_PER_STEP = 8
MOE_TILE_ROWS = 256
MOE_OUT_CHUNKS = 4
ROUTER_LANES = 128


def _cparams(n_axes):
    return pltpu.CompilerParams(dimension_semantics=("arbitrary",) * n_axes,
                                vmem_limit_bytes=VMEM_LIMIT_BYTES)


def _matmul_rows(n):
    for d in range(min(n, MATMUL_MAX_ROWS), 0, -1):
        if n % d == 0 and d % BF16_SUBLANES == 0:
            return d
    raise ValueError(f"no row tile for {n} rows")


def _silu(x):
    return x * jax.nn.sigmoid(x)


def _ada_kernel(c_ref, w_ref, b_ref, os_ref, op_ref):
    s = _silu(c_ref[...]).astype(BF16)
    mod = jnp.dot(s, w_ref[0].astype(BF16), preferred_element_type=F32) + b_ref[0]
    n_s, n_p = os_ref.shape[1], op_ref.shape[1]
    os_ref[0] = mod[:n_s]
    op_ref[0] = mod[n_s:n_s + n_p]


def _ada(c_rows, n_s, n_p, w_ada, b_ada):
    depth, d, n6 = w_ada.shape
    r = c_rows.shape[0]
    tn = min(1024, d)
    assert n6 % tn == 0 and n_s % 8 == 0 and n_p % 8 == 0 and r >= n_s + n_p
    return pl.pallas_call(
        _ada_kernel,
        out_shape=[jax.ShapeDtypeStruct((depth, n_s, n6), F32),
                   jax.ShapeDtypeStruct((depth, n_p, n6), F32)],
        grid=(depth, n6 // tn),
        in_specs=[pl.BlockSpec((r, d), lambda l, j: (0, 0)),
                  pl.BlockSpec((1, d, tn), lambda l, j: (l, 0, j)),
                  pl.BlockSpec((1, 1, tn), lambda l, j: (l, 0, j))],
        out_specs=[pl.BlockSpec((1, n_s, tn), lambda l, j: (l, 0, j)),
                   pl.BlockSpec((1, n_p, tn), lambda l, j: (l, 0, j))],
        name="adaln_mod",
        compiler_params=_cparams(2),
    )(c_rows, w_ada, b_ada.reshape(depth, 1, n6))


def _route(logits, n_groups, per_group):
    lane = lax.broadcasted_iota(I32, logits.shape, 1).astype(F32)
    big = float(ROUTER_LANES)
    ninf = -jnp.inf
    gl = jnp.where(lane < n_groups, logits, ninf)
    gmax = jnp.max(gl, axis=1, keepdims=True)
    gsel = jnp.min(jnp.where(gl == gmax, lane, big), axis=1, keepdims=True)
    gden = jnp.sum(jnp.exp(gl - gmax), axis=1, keepdims=True)
    gw = 1.0 / gden
    lo = n_groups + gsel * per_group
    el = jnp.where(jnp.logical_and(lane >= lo, lane < lo + per_group), logits, ninf)
    v1 = jnp.max(el, axis=1, keepdims=True)
    i1 = jnp.min(jnp.where(el == v1, lane, big), axis=1, keepdims=True)
    el2 = jnp.where(lane == i1, ninf, el)
    v2 = jnp.max(el2, axis=1, keepdims=True)
    i2 = jnp.min(jnp.where(el2 == v2, lane, big), axis=1, keepdims=True)
    t = jnp.exp(v2 - v1)
    w1 = gw / (1.0 + t)
    w2 = gw * t / (1.0 + t)
    out = jnp.where(lane == 0, i1 - n_groups,
                    jnp.where(lane == 1, i2 - n_groups,
                              jnp.where(lane == 2, w1, jnp.where(lane == 3, w2, 0.0))))
    return out


def _norm_kernel(*refs, add_lanes, has_mod, emit_x, h_dtype, route_dims, final, n_ptiles,
                 n_srows, tiles_per_seq):
    n_add = len(add_lanes)
    weighted = any(l is not None for l in add_lanes)
    it = iter(refs)
    xp_ref, xs_ref = next(it), next(it)
    add_refs = [next(it) for _ in range(n_add)]
    if weighted:
        rw_ref = next(it)
    if n_add:
        gp_ref, gs_ref = next(it), next(it)
    gain_ref = next(it)
    if has_mod:
        scp_ref, scs_ref, shp_ref, shs_ref = next(it), next(it), next(it), next(it)
    if route_dims:
        wr_ref = next(it)
    if emit_x:
        xpo_ref, xso_ref = next(it), next(it)
    if h_dtype is not None:
        h_ref = next(it)
    if route_dims:
        r_ref = next(it)
    if final:
        yp_ref, ys_ref = next(it), next(it)

    def compute(x, adds, rw, gate, sc, sh):
        if adds:
            scaled = [a if l is None else a * rw[:, l:l + 1] for a, l in zip(adds, add_lanes)]
            s = scaled[0]
            for a in scaled[1:]:
                s = s + a
            x = x + gate * s
        h = x * lax.rsqrt(jnp.mean(x * x, axis=-1, keepdims=True) + EPS) * gain_ref[...]
        if has_mod:
            h = h * (1.0 + sc) + sh
        return x, h

    def logits_of(h):
        return jnp.dot(h, wr_ref[...], preferred_element_type=F32,
                       precision=lax.Precision.HIGHEST)

    i = pl.program_id(0)

    @pl.when(i < n_ptiles)
    def _():
        seq_row = pl.ds(i // tiles_per_seq, 1)
        adds = [a[...].astype(F32) for a in add_refs]
        x, h = compute(xp_ref[...], adds, rw_ref[...] if weighted else None,
                       gp_ref[seq_row, :] if n_add else None,
                       scp_ref[seq_row, :] if has_mod else None,
                       shp_ref[seq_row, :] if has_mod else None)
        if emit_x:
            xpo_ref[...] = x
        if h_dtype is not None:
            h_ref[...] = h.astype(h_dtype)
        if route_dims:
            r_ref[...] = _route(logits_of(h), *route_dims)
        if final:
            yp_ref[...] = h

    @pl.when(i == n_ptiles)
    def _():
        adds = [a[0:n_srows].astype(F32) for a in add_refs]
        x, h = compute(xs_ref[...], adds, rw_ref[0:n_srows] if weighted else None,
                       gs_ref[...] if n_add else None,
                       scs_ref[...] if has_mod else None,
                       shs_ref[...] if has_mod else None)
        if emit_x:
            xso_ref[...] = x
        if h_dtype is not None:
            h_ref[0:n_srows] = h.astype(h_dtype)
        if route_dims:
            r_ref[0:n_srows] = _route(logits_of(h), *route_dims)
        if final:
            ys_ref[...] = h


def _norm_call(xp, xs, seq_len, adds, gate, gain, mod, *, emit_x, name, h_dtype=None,
               route=None, route_w=None, final=False):
    n_p, d = xp.shape
    n_s = xs.shape[0]
    te = min(ELEM_TILE_ROWS, seq_len)
    assert seq_len % te == 0 and n_s <= te and n_s % 8 == 0
    tiles_per_seq = seq_len // te
    n_ptiles = n_p // te
    n_tot = n_p + n_s
    mod_p_rows = (gate if gate is not None else mod)[0].shape[1]

    def pidx(i):
        return jnp.minimum(i, n_ptiles - 1)

    def modp_spec(layer, chunk):
        return pl.BlockSpec((None, mod_p_rows, d), lambda i: (layer, 0, chunk))

    def mods_spec(layer, chunk):
        return pl.BlockSpec((None, n_s, d), lambda i: (layer, 0, chunk))

    args = [xp, xs]
    specs = [pl.BlockSpec((te, d), lambda i: (pidx(i), 0)),
             pl.BlockSpec((n_s, d), lambda i: (0, 0))]
    for arr, plane, _ in adds:
        args.append(arr)
        if plane is None:
            specs.append(pl.BlockSpec((te, d), lambda i: (i, 0)))
        else:
            specs.append(pl.BlockSpec((None, te, d), lambda i, plane=plane: (plane, i, 0)))
    if route_w is not None:
        args.append(route_w)
        specs.append(pl.BlockSpec((te, ROUTER_LANES), lambda i: (i, 0)))
    if adds:
        mp, ms, layer, chunk = gate
        args += [mp, ms]
        specs += [modp_spec(layer, chunk), mods_spec(layer, chunk)]
    args.append(gain)
    specs.append(pl.BlockSpec((1, d), lambda i: (0, 0)))
    if mod is not None:
        mp, ms, layer, sc_chunk, sh_chunk = mod
        args += [mp, ms, mp, ms]
        specs += [modp_spec(layer, sc_chunk), mods_spec(layer, sc_chunk),
                  modp_spec(layer, sh_chunk), mods_spec(layer, sh_chunk)]
    route_dims = None
    if route is not None:
        wr, n_groups, per_group = route
        route_dims = (n_groups, per_group)
        args.append(wr)
        specs.append(pl.BlockSpec(wr.shape, lambda i: (0, 0)))

    out_shapes, out_specs = [], []
    if emit_x:
        out_shapes += [jax.ShapeDtypeStruct((n_p, d), F32), jax.ShapeDtypeStruct((n_s, d), F32)]
        out_specs += [pl.BlockSpec((te, d), lambda i: (pidx(i), 0)),
                      pl.BlockSpec((n_s, d), lambda i: (0, 0))]
    if h_dtype is not None:
        out_shapes.append(jax.ShapeDtypeStruct((n_tot, d), h_dtype))
        out_specs.append(pl.BlockSpec((te, d), lambda i: (i, 0)))
    if route is not None:
        out_shapes.append(jax.ShapeDtypeStruct((n_tot, ROUTER_LANES), F32))
        out_specs.append(pl.BlockSpec((te, ROUTER_LANES), lambda i: (i, 0)))
    if final:
        out_shapes += [jax.ShapeDtypeStruct((n_p, d), F32), jax.ShapeDtypeStruct((n_s, d), F32)]
        out_specs += [pl.BlockSpec((te, d), lambda i: (pidx(i), 0)),
                      pl.BlockSpec((n_s, d), lambda i: (0, 0))]

    kern = functools.partial(_norm_kernel, add_lanes=tuple(l for _, _, l in adds),
                             has_mod=mod is not None, emit_x=emit_x, h_dtype=h_dtype,
                             route_dims=route_dims, final=final, n_ptiles=n_ptiles, n_srows=n_s,
                             tiles_per_seq=tiles_per_seq)
    return pl.pallas_call(
        kern, out_shape=out_shapes, grid=(n_ptiles + 1,),
        in_specs=specs, out_specs=out_specs, name=name, compiler_params=_cparams(1),
    )(*args)


def _mm_kernel(x_ref, w_ref, o_ref):
    o_ref[...] = jnp.dot(x_ref[...], w_ref[...].astype(BF16),
                         preferred_element_type=F32).astype(o_ref.dtype)


def _matmul(x, w, layer, tn, name, out_dtype=BF16):
    m, k = x.shape
    n = w.shape[2]
    tm = _matmul_rows(m)
    assert n % tn == 0
    return pl.pallas_call(
        _mm_kernel,
        out_shape=jax.ShapeDtypeStruct((m, n), out_dtype),
        grid=(m // tm, n // tn),
        in_specs=[pl.BlockSpec((tm, k), lambda i, j: (i, 0)),
                  pl.BlockSpec((None, k, tn), lambda i, j: (layer, 0, j))],
        out_specs=pl.BlockSpec((tm, tn), lambda i, j: (i, j)),
        name=name,
        compiler_params=_cparams(2),
    )(x, w)


def _merge_kernel(a_ref, r_ref, g0_ref, g1_ref, wa_ref, wb_ref, o_ref):
    ya = jnp.dot(a_ref[...], wa_ref[...].astype(BF16), preferred_element_type=F32)
    yb = jnp.dot(r_ref[...], wb_ref[...].astype(BF16), preferred_element_type=F32)
    g0 = jax.nn.sigmoid(g0_ref[...].astype(F32))
    g1 = jax.nn.sigmoid(g1_ref[...].astype(F32))
    o_ref[...] = (g0 * ya + g1 * yb).astype(o_ref.dtype)


def _merge(a, rn, proj, gate_col0, w_a, w_b, layer, tn):
    m, ka = a.shape
    kb = rn.shape[1]
    d = w_a.shape[2]
    tm = _matmul_rows(m)
    assert d % tn == 0 and gate_col0 % tn == 0
    c0 = gate_col0 // tn
    c1 = (gate_col0 + d) // tn
    return pl.pallas_call(
        _merge_kernel,
        out_shape=jax.ShapeDtypeStruct((m, d), BF16),
        grid=(m // tm, d // tn),
        in_specs=[pl.BlockSpec((tm, ka), lambda i, j: (i, 0)),
                  pl.BlockSpec((tm, kb), lambda i, j: (i, 0)),
                  pl.BlockSpec((tm, tn), lambda i, j: (i, c0 + j)),
                  pl.BlockSpec((tm, tn), lambda i, j: (i, c1 + j)),
                  pl.BlockSpec((None, ka, tn), lambda i, j: (layer, 0, j)),
                  pl.BlockSpec((None, kb, tn), lambda i, j: (layer, 0, j))],
        out_specs=pl.BlockSpec((tm, tn), lambda i, j: (i, j)),
        name="branch_merge",
        compiler_params=_cparams(2),
    )(a, rn, proj, proj, w_a, w_b)


def _rope_tables(pos, rot_dim, head_dim, base):
    half = rot_dim // 2
    inv = 1.0 / (base ** (jnp.arange(half, dtype=F32) * (2.0 / rot_dim)))
    ang = pos.astype(F32)[:, None] * inv[None, :]
    cos, sin = jnp.cos(ang), jnp.sin(ang)
    t = pos.shape[0]
    rest = head_dim - rot_dim
    cos_h = jnp.concatenate([cos, cos, jnp.ones((t, rest), F32)], axis=1)
    sin_lo = jnp.concatenate([jnp.zeros((t, half), F32), sin, jnp.zeros((t, rest), F32)], axis=1)
    sin_hi = jnp.concatenate([-sin, jnp.zeros((t, half + rest), F32)], axis=1)
    reps = LANES // head_dim
    tile = lambda a: jnp.tile(a, (1, reps))
    return tile(cos_h), tile(sin_lo), tile(sin_hi)


def _rope_lanes(x, cos, sin_lo, sin_hi, half):
    return (x * cos + pltpu.roll(x, half, axis=1) * sin_lo
            + pltpu.roll(x, LANES - half, axis=1) * sin_hi)


def _rope_wide(x, cos, sin_lo, sin_hi, half):
    cols = [_rope_lanes(x[:, c:c + LANES], cos, sin_lo, sin_hi, half)
            for c in range(0, x.shape[1], LANES)]
    return cols[0] if len(cols) == 1 else jnp.concatenate(cols, axis=1)


def _dot_nt(a, b):
    return lax.dot_general(a, b, (((1,), (1,)), ((), ())), preferred_element_type=F32)


def _dot_tn(a, b):
    return lax.dot_general(a, b, (((0,), (0,)), ((), ())), preferred_element_type=F32)


def _sink_softmax_pv(score_parts, value_parts, sink_col):
    m = sink_col
    for s in score_parts:
        m = jnp.maximum(m, jnp.max(s, axis=1, keepdims=True))
    den = jnp.exp(sink_col - m)
    acc = None
    for s, v in zip(score_parts, value_parts):
        p = jnp.exp(s - m)
        den = den + jnp.sum(p, axis=1, keepdims=True)
        pv = jnp.dot(p.astype(BF16), v, preferred_element_type=F32)
        acc = pv if acc is None else acc + pv
    return acc * (1.0 / den)


def _swa_prompt_kernel(sink_ref, q_ref, kc_ref, kp_ref, vc_ref, vp_ref,
                       cc_ref, lc_ref, hc_ref, cp_ref, lp_ref, hp_ref, bias_ref,
                       a_ref, nk_ref, nv_ref, *, layer, n_heads, n_kv, hd, half):
    w = q_ref.shape[0]
    grp = n_heads // n_kv
    per_tile = LANES // hd
    tabs_c = (cc_ref[...], lc_ref[...], hc_ref[...])
    tabs_p = (cp_ref[...], lp_ref[...], hp_ref[...])
    q = _rope_wide(q_ref[...].astype(F32), *tabs_c, half) * (hd ** -0.5)
    kc_f = _rope_wide(kc_ref[...].astype(F32), *tabs_c, half)
    kp_f = _rope_wide(kp_ref[...].astype(F32), *tabs_p, half)
    vc_f, vp_f = vc_ref[...].astype(F32), vp_ref[...].astype(F32)
    lane = lax.broadcasted_iota(I32, (w, LANES), 1)
    low = lane < hd
    upper = lax.broadcasted_iota(I32, (w, w), 1) > lax.broadcasted_iota(I32, (w, w), 0)
    bias = bias_ref[...]

    def both_halves(x, g):
        t = (g * hd) // LANES
        slab = x[:, t * LANES:(t + 1) * LANES]
        other = pltpu.roll(slab, hd, axis=1)
        return jnp.where(low, slab, other) if (g % per_tile) == 0 else jnp.where(low, other, slab)

    slabs = []
    for g in range(n_kv):
        kk = jnp.concatenate([both_halves(kp_f, g), both_halves(kc_f, g)], axis=0).astype(BF16)
        vv = jnp.concatenate([both_halves(vp_f, g), both_halves(vc_f, g)], axis=0).astype(BF16)
        heads = [g * grp + j for j in range(grp)]
        qg = jnp.concatenate(
            [jnp.where(low if h % per_tile == 0 else jnp.logical_not(low),
                       q[:, (h // per_tile) * LANES:(h // per_tile + 1) * LANES], 0.0)
             for h in heads], axis=0).astype(BF16)
        s_all = _dot_nt(qg, kk)
        probs, inv_den = [], []
        for j, h in enumerate(heads):
            s = jnp.where(upper, s_all[j * w:(j + 1) * w, :w], s_all[j * w:(j + 1) * w, w:]) + bias
            sink = sink_ref[layer, h]
            m = jnp.maximum(jnp.max(s, axis=1, keepdims=True), sink)
            p = jnp.exp(s - m)
            inv_den.append(1.0 / (jnp.sum(p, axis=1, keepdims=True) + jnp.exp(sink - m)))
            probs.append(jnp.concatenate([jnp.where(upper, p, 0.0), jnp.where(upper, 0.0, p)],
                                         axis=1).astype(BF16))
        o = jnp.dot(jnp.concatenate(probs, axis=0), vv, preferred_element_type=F32)
        o = [o[j * w:(j + 1) * w] * inv_den[j] for j in range(grp)]
        for j in range(0, grp, per_tile):
            slabs.append(jnp.where(low, o[j], o[j + 1]))
    a_ref[...] = jnp.concatenate(slabs, axis=1).astype(a_ref.dtype)

    @pl.when(pl.program_id(1) == pl.num_programs(1) - 1)
    def _():
        nk_ref[...] = kc_f
        nv_ref[...] = vc_f


def _swa_prompt(proj, sinks, layer, bp, seq_len, n_rows, n_heads, n_kv, hd, tabs, bias):
    w = WINDOW
    nb = seq_len // w
    qw, kvw = n_heads * hd, n_kv * hd
    assert qw % kvw == 0 and 2 * hd == LANES and (n_heads // n_kv) % 2 == 0
    kcol, vcol = qw // kvw, qw // kvw + 1
    cur = lambda b, n: b * nb + n
    prev = lambda b, n: b * nb + jnp.maximum(n - 1, 0)
    tab_cur = pl.BlockSpec((w, LANES), lambda b, n: (n, 0))
    tab_prev = pl.BlockSpec((w, LANES), lambda b, n: (jnp.maximum(n - 1, 0), 0))
    kern = functools.partial(_swa_prompt_kernel, layer=layer, n_heads=n_heads, n_kv=n_kv,
                             hd=hd, half=hd // 8)
    return pl.pallas_call(
        kern,
        out_shape=[jax.ShapeDtypeStruct((n_rows, qw), BF16),
                   jax.ShapeDtypeStruct((bp, w, kvw), F32),
                   jax.ShapeDtypeStruct((bp, w, kvw), F32)],
        grid=(bp, nb),
        in_specs=[pl.BlockSpec(memory_space=pltpu.SMEM),
                  pl.BlockSpec((w, qw), lambda b, n: (cur(b, n), 0)),
                  pl.BlockSpec((w, kvw), lambda b, n: (cur(b, n), kcol)),
                  pl.BlockSpec((w, kvw), lambda b, n: (prev(b, n), kcol)),
                  pl.BlockSpec((w, kvw), lambda b, n: (cur(b, n), vcol)),
                  pl.BlockSpec((w, kvw), lambda b, n: (prev(b, n), vcol)),
                  tab_cur, tab_cur, tab_cur, tab_prev, tab_prev, tab_prev,
                  pl.BlockSpec((None, w, w), lambda b, n: (jnp.minimum(n, 1), 0, 0))],
        out_specs=[pl.BlockSpec((w, qw), lambda b, n: (cur(b, n), 0)),
                   pl.BlockSpec((None, w, kvw), lambda b, n: (b, 0, 0)),
                   pl.BlockSpec((None, w, kvw), lambda b, n: (b, 0, 0))],
        name="swa_prompt",
        compiler_params=_cparams(2),
    )(sinks, proj, proj, proj, proj, proj, *tabs, *tabs, bias)


def _swa_sample_kernel(sink_ref, a_in_ref, q_ref, k_ref, v_ref, kb_ref, vb_ref,
                       c_ref, l_ref, h_ref, bias_c_ref, bias_n_ref,
                       a_ref, nk_ref, nv_ref, *, layer, n_heads, n_kv, hd, half):
    del a_in_ref
    rows = q_ref.shape[0]
    grp = n_heads // n_kv
    tabs = (c_ref[...], l_ref[...], h_ref[...])
    q = (_rope_wide(q_ref[...].astype(F32), *tabs, half) * (hd ** -0.5)).astype(BF16)
    kn_f = _rope_wide(k_ref[...].astype(F32), *tabs, half)
    kn = kn_f.astype(BF16)
    vn = v_ref[...]
    kb = kb_ref[...].astype(BF16)
    vb = vb_ref[...].astype(BF16)
    bias_c = bias_c_ref[...]
    bias_n = bias_n_ref[...]
    outs = []
    for g in range(n_kv):
        ks = slice(g * hd, (g + 1) * hd)
        qg = jnp.concatenate([q[:, (g * grp + j) * hd:(g * grp + j + 1) * hd]
                              for j in range(grp)], axis=0)
        sink = jnp.concatenate([jnp.full((rows, 1), sink_ref[layer, g * grp + j], F32)
                                for j in range(grp)], axis=0)
        s_c = _dot_nt(qg, kb[:, ks]) + bias_c
        s_n = _dot_nt(qg, kn[:, ks]) + bias_n
        o = _sink_softmax_pv([s_c, s_n], [vb[:, ks], vn[:, ks]], sink)
        outs += [o[j * rows:(j + 1) * rows] for j in range(grp)]
    a_ref[...] = jnp.concatenate(outs, axis=1).astype(a_ref.dtype)
    nk_ref[...] = kn_f
    nv_ref[...] = vn.astype(F32)


def _swa_sample(proj, a, sinks, cache_k, cache_v, layer, row0, bs, ts, n_heads, n_kv, hd,
                tabs, bias_c, bias_n):
    sg = SAMPLE_SEQS_PER_STEP
    rows = sg * ts
    wb = cache_k.shape[1] // bs
    qw, kvw = n_heads * hd, n_kv * hd
    assert bs % sg == 0 and row0 % rows == 0 and rows % BF16_SUBLANES == 0
    r0 = row0 // rows
    kcol, vcol = qw // kvw, qw // kvw + 1
    full = lambda shape: pl.BlockSpec(shape, lambda s: (0,) * len(shape))
    kern = functools.partial(_swa_sample_kernel, layer=layer, n_heads=n_heads, n_kv=n_kv,
                             hd=hd, half=hd // 8)
    return pl.pallas_call(
        kern,
        out_shape=[jax.ShapeDtypeStruct(a.shape, a.dtype),
                   jax.ShapeDtypeStruct((bs * ts, kvw), F32),
                   jax.ShapeDtypeStruct((bs * ts, kvw), F32)],
        grid=(bs // sg,),
        in_specs=[pl.BlockSpec(memory_space=pltpu.SMEM),
                  pl.BlockSpec(memory_space=pl.ANY),
                  pl.BlockSpec((rows, qw), lambda s: (r0 + s, 0)),
                  pl.BlockSpec((rows, kvw), lambda s: (r0 + s, kcol)),
                  pl.BlockSpec((rows, kvw), lambda s: (r0 + s, vcol)),
                  pl.BlockSpec((None, sg * wb, kvw), lambda s: (layer, s, 0)),
                  pl.BlockSpec((None, sg * wb, kvw), lambda s: (layer, s, 0)),
                  full((rows, LANES)), full((rows, LANES)), full((rows, LANES)),
                  full(bias_c.shape), full(bias_n.shape)],
        out_specs=[pl.BlockSpec((rows, qw), lambda s: (r0 + s, 0)),
                   pl.BlockSpec((rows, kvw), lambda s: (s, 0)),
                   pl.BlockSpec((rows, kvw), lambda s: (s, 0))],
        input_output_aliases={1: 0},
        name="swa_sample",
        compiler_params=_cparams(1),
    )(sinks, a, proj, proj, proj, cache_k, cache_v, *tabs, bias_c, bias_n)


def _ret_head(q_raw, k_raw, cos, sin, scale):
    half = LANES // 2
    q = q_raw * cos + pltpu.roll(q_raw, half, axis=1) * sin
    k = (k_raw * cos + pltpu.roll(k_raw, half, axis=1) * sin) * scale
    return q, k


def _ret_finish(o, g_raw):
    n = o * lax.rsqrt(jnp.mean(o * o, axis=-1, keepdims=True) + EPS)
    return n * _silu(g_raw)


def _ret_prompt_kernel(*refs, heads, heads_per_block, scale):
    nb = heads // heads_per_block
    q_refs, k_refs, v_refs, g_refs = (refs[i * nb:(i + 1) * nb] for i in range(4))
    (cos_ref, sin_ref, intra_ref, qd_ref, kd_ref, cd_ref, rn_ref, st_ref, s_ref) = refs[4 * nb:]
    c = pl.program_id(1)

    @pl.when(c == 0)
    def _():
        s_ref[...] = jnp.zeros_like(s_ref)

    cos, sin = cos_ref[...], sin_ref[...]
    for hh in range(heads):
        blk = hh // heads_per_block
        sl = slice((hh % heads_per_block) * LANES, (hh % heads_per_block + 1) * LANES)
        q, k = _ret_head(q_refs[blk][:, sl].astype(F32), k_refs[blk][:, sl].astype(F32),
                         cos, sin, scale)
        v = v_refs[blk][:, sl]
        att = _dot_nt(q.astype(BF16), k.astype(BF16)) * intra_ref[hh]
        s0 = s_ref[hh]
        o = (jnp.dot(att.astype(BF16), v, preferred_element_type=F32)
             + jnp.dot((q * qd_ref[hh]).astype(BF16), s0.astype(BF16),
                       preferred_element_type=F32))
        s_ref[hh] = s0 * cd_ref[hh] + _dot_tn((k * kd_ref[hh]).astype(BF16), v)
        rn_ref[:, hh * LANES:(hh + 1) * LANES] = _ret_finish(
            o, g_refs[blk][:, sl].astype(F32)).astype(rn_ref.dtype)

    @pl.when(c == pl.num_programs(1) - 1)
    def _():
        st_ref[...] = s_ref[...]


def _ret_prompt(proj, bp, seq_len, n_rows, col0, n_heads, tabs, dec):
    ch = RET_CHUNK
    nc = seq_len // ch
    hw = n_heads * LANES
    hg = 4 if n_heads % 4 == 0 else n_heads
    bw = hg * LANES
    nb = n_heads // hg
    assert col0 % bw == 0 and hw % bw == 0
    inspecs = [pl.BlockSpec((ch, bw), lambda b, c, cb=(col0 + k * hw) // bw + j: (b * nc + c, cb))
               for k in range(4) for j in range(nb)]
    tab = pl.BlockSpec((ch, LANES), lambda b, c: (c, 0))
    dspec = pl.BlockSpec((n_heads, ch, LANES), lambda b, c: (0, 0, 0))
    kern = functools.partial(_ret_prompt_kernel, heads=n_heads, heads_per_block=hg,
                             scale=LANES ** -0.5)
    return pl.pallas_call(
        kern,
        out_shape=[jax.ShapeDtypeStruct((n_rows, hw), BF16),
                   jax.ShapeDtypeStruct((bp, n_heads, LANES, LANES), F32)],
        grid=(bp, nc),
        in_specs=inspecs + [tab, tab, dspec, dspec, dspec, dspec],
        out_specs=[pl.BlockSpec((ch, hw), lambda b, c: (b * nc + c, 0)),
                   pl.BlockSpec((None, n_heads, LANES, LANES), lambda b, c: (b, 0, 0, 0))],
        scratch_shapes=[pltpu.VMEM((n_heads, LANES, LANES), F32)],
        name="ret_prompt",
        compiler_params=_cparams(2),
    )(*([proj] * (4 * nb)), *tabs, *dec)


def _ret_sample_kernel(rn_in_ref, q_ref, k_ref, v_ref, g_ref, s_in_ref, cos_ref, sin_ref,
                       intra_ref, qd_ref, kd_ref, cd_ref, sel_ref,
                       rn_ref, s_out_ref, *, heads, seqs, scale):
    del rn_in_ref
    cos, sin = cos_ref[...], sin_ref[...]
    for hh in range(heads):
        sl = slice(hh * LANES, (hh + 1) * LANES)
        q, k = _ret_head(q_ref[:, sl].astype(F32), k_ref[:, sl].astype(F32), cos, sin, scale)
        v = v_ref[:, sl]
        att = _dot_nt(q.astype(BF16), k.astype(BF16)) * intra_ref[hh]
        o = jnp.dot(att.astype(BF16), v, preferred_element_type=F32)
        qd = (q * qd_ref[hh]).astype(BF16)
        kd = k * kd_ref[hh]
        for s in range(seqs):
            sel = sel_ref[s]
            s0 = s_in_ref[s, hh]
            o = o + sel * jnp.dot(qd, s0.astype(BF16), preferred_element_type=F32)
            s_out_ref[s, hh] = s0 * cd_ref[hh] + _dot_tn((kd * sel).astype(BF16), v)
        rn_ref[:, sl] = _ret_finish(o, g_ref[:, sl].astype(F32)).astype(rn_ref.dtype)


def _ret_sample(proj, rn, state, layer, row0, bs, ts, col0, n_heads, tabs, dec, sel):
    sg = SAMPLE_SEQS_PER_STEP
    rows = sg * ts
    hw = n_heads * LANES
    hg = 4 if n_heads % 4 == 0 else n_heads
    bw = hg * LANES
    nhalf = n_heads // hg
    assert bs % sg == 0 and row0 % rows == 0 and col0 % bw == 0
    r0 = row0 // rows
    cb = lambda k: (col0 + k * hw) // bw
    inspec = lambda k: pl.BlockSpec((rows, bw), lambda s, hf: (r0 + s, cb(k) + hf))
    tab = pl.BlockSpec((rows, LANES), lambda s, hf: (0, 0))
    dspec = lambda a: pl.BlockSpec((hg,) + a.shape[1:], lambda s, hf: (hf, 0, 0))
    st_spec = pl.BlockSpec((None, sg, hg, LANES, LANES), lambda s, hf: (layer, s, hf, 0, 0))
    kern = functools.partial(_ret_sample_kernel, heads=hg, seqs=sg, scale=LANES ** -0.5)
    return pl.pallas_call(
        kern,
        out_shape=[jax.ShapeDtypeStruct(rn.shape, rn.dtype),
                   jax.ShapeDtypeStruct(state.shape[1:], F32)],
        grid=(bs // sg, nhalf),
        in_specs=[pl.BlockSpec(memory_space=pl.ANY),
                  inspec(0), inspec(1), inspec(2), inspec(3), st_spec, tab, tab,
                  dspec(dec[0]), dspec(dec[1]), dspec(dec[2]), dspec(dec[3]),
                  pl.BlockSpec(sel.shape, lambda s, hf: (0, 0, 0))],
        out_specs=[pl.BlockSpec((rows, bw), lambda s, hf: (r0 + s, hf)),
                   pl.BlockSpec((sg, hg, LANES, LANES), lambda s, hf: (s, hf, 0, 0))],
        input_output_aliases={0: 0},
        name="ret_sample",
        compiler_params=_cparams(2),
    )(rn, proj, proj, proj, proj, state, *tabs, *dec, sel)


def _ret_decay_tables(n_heads, chunk, rows):
    log_g = jnp.log(1.0 - jnp.exp2(-5.0 - jnp.arange(n_heads, dtype=F32)))
    i = jnp.arange(chunk, dtype=F32)
    diff = i[:, None] - i[None, :]
    intra = jnp.where(diff >= 0, jnp.exp(jnp.maximum(diff, 0.0) * log_g[:, None, None]), 0.0)
    q_dec = jnp.exp((i + 1.0) * log_g[:, None])
    k_dec = jnp.exp((chunk - 1.0 - i) * log_g[:, None])
    c_dec = jnp.exp(chunk * log_g)
    reps = rows // chunk
    same = jnp.kron(jnp.eye(reps, dtype=F32), jnp.ones((chunk, chunk), F32))
    intra_t = jnp.tile(intra, (1, reps, reps)) * same[None]
    qd_t = jnp.broadcast_to(jnp.tile(q_dec, (1, reps))[:, :, None], (n_heads, rows, LANES))
    kd_t = jnp.broadcast_to(jnp.tile(k_dec, (1, reps))[:, :, None], (n_heads, rows, LANES))
    cd_t = jnp.broadcast_to(c_dec[:, None, None], (n_heads, LANES, LANES))
    return intra_t, qd_t, kd_t, cd_t


def _ret_rope_tables(pos):
    half = LANES // 2
    inv = 1.0 / (RET_ROT_BASE ** (jnp.arange(half, dtype=F32) * (2.0 / LANES)))
    ang = pos.astype(F32)[:, None] * inv[None, :]
    cos, sin = jnp.cos(ang), jnp.sin(ang)
    return jnp.concatenate([cos, cos], axis=1), jnp.concatenate([-sin, sin], axis=1)


def _moe_plan(route_out, n_tok, n_exp, tm, plane_rows):
    n_asg = 2 * n_tok
    n_tiles = n_asg // tm
    e_flat = route_out[:, 0:2].astype(I32).reshape(n_asg)
    order = jnp.argsort(e_flat, stable=True).astype(I32)
    tok = order >> 1
    dst = (order & 1) * plane_rows + tok
    eid = jnp.arange(n_exp, dtype=I32)
    shift = tm.bit_length() - 1
    tri = (eid[None, :] <= eid[:, None]).astype(I32)
    cnt = jnp.sum((e_flat[:, None] == eid[None, :]).astype(I32), axis=0)
    cend = jnp.sum(tri * cnt[None, :], axis=1)
    cstart = cend - cnt
    first_tile = cstart >> shift
    n_e = jnp.where(cnt > 0, ((cend - 1) >> shift) - first_tile + 1, 0)
    pend = jnp.sum(tri * n_e[None, :], axis=1)
    pstart = pend - n_e
    n_steps = jnp.sum(n_e)
    s = jnp.minimum(jnp.arange(n_tiles + n_exp, dtype=I32), n_steps - 1)
    se = jnp.sum((pend[None, :] <= s[:, None]).astype(I32), axis=1)
    onehot = (se[:, None] == eid[None, :]).astype(I32)
    pick = lambda table: jnp.sum(onehot * table[None, :], axis=1)
    st = pick(first_tile) + s - pick(pstart)
    lo = jnp.clip(pick(cstart) - st * tm, 0, tm)
    hi = jnp.clip(pick(cend) - st * tm, 0, tm)
    as_i32 = lambda a: a.astype(I32)
    return (as_i32(se), as_i32(st), as_i32(lo), as_i32(hi), as_i32(n_steps).reshape(1),
            as_i32(tok), as_i32(dst))


def _moe_kernel(se_ref, st_ref, lo_ref, hi_ref, ns_ref, tok_ref, dst_ref,
                h_hbm, wg_ref, wu_ref, wd_ref,
                y_hbm,
                gbuf, obuf, wg_bf, wu_bf, wd_bf, gsem, ssem, *, n_tiles):
    s = pl.program_id(0)
    n_steps = ns_ref[0]
    tm, d = obuf.shape[1], obuf.shape[2]
    ff = wg_bf.shape[1]

    def gather_row(tile, slot, r):
        pltpu.make_async_copy(h_hbm.at[pl.ds(tok_ref[tile * tm + r], 1)],
                              gbuf.at[slot, pl.ds(r, 1)], gsem.at[slot]).start()

    def gather_wait(slot):
        pltpu.make_async_copy(h_hbm.at[pl.ds(0, tm)], gbuf.at[slot], gsem.at[slot]).wait()

    def scatter_row(tile, slot, r):
        pltpu.make_async_copy(obuf.at[slot, pl.ds(r, 1)],
                              y_hbm.at[pl.ds(dst_ref[tile * tm + r], 1)], ssem.at[slot]).start()

    def scatter_wait(slot):
        pltpu.make_async_copy(obuf.at[slot], y_hbm.at[pl.ds(0, tm)], ssem.at[slot]).wait()

    def experts(slot, lo, hi, accumulate, row_dmas):
        n_chunks = MOE_OUT_CHUNKS
        cw = d // n_chunks
        n_gaps = 3 + n_chunks
        per_gap = -(-len(row_dmas) // n_gaps)

        def issue(k):
            for thunk in row_dmas[k * per_gap:(k + 1) * per_gap]:
                thunk()

        x = gbuf[slot].astype(BF16)
        issue(0)
        g = jnp.dot(x, wg_bf[...], preferred_element_type=F32)
        issue(1)
        u = jnp.dot(x, wu_bf[...], preferred_element_type=F32)
        issue(2)
        row = lax.broadcasted_iota(I32, (tm, ff), 0)
        owned = jnp.logical_and(row >= lo, row < hi)
        hid = jnp.where(owned, _silu(g) * u, 0.0).astype(BF16)
        for c in range(n_chunks):
            cols = slice(c * cw, (c + 1) * cw)
            y = jnp.dot(hid, wd_bf[:, cols], preferred_element_type=F32)
            if accumulate:
                obuf[slot, :, cols] += y
            else:
                obuf[slot, :, cols] = y
            issue(3 + c)

    @pl.when(s < n_steps)
    def _():
        tile = st_ref[s]
        slot = lax.rem(tile, 2)
        prev = jnp.maximum(s - 1, 0)
        first = jnp.logical_or(s == 0, tile != st_ref[prev])
        expert_changed = jnp.logical_or(s == 0, se_ref[s] != se_ref[prev])
        lo, hi = lo_ref[s], hi_ref[s]
        nxt = jnp.minimum(tile + 1, n_tiles - 1)
        gathers = [functools.partial(gather_row, nxt, 1 - slot, r) for r in range(tm)]
        scatters = [functools.partial(scatter_row, tile - 1, 1 - slot, r) for r in range(tm)]
        both = [t for pair in zip(gathers, scatters) for t in pair]

        @pl.when(s == 0)
        def _():
            for r in range(tm):
                gather_row(0, 0, r)

        @pl.when(first)
        def _():
            gather_wait(slot)

        @pl.when(jnp.logical_and(first, tile >= 2))
        def _():
            scatter_wait(slot)

        @pl.when(expert_changed)
        def _():
            wg_bf[...] = wg_ref[...].astype(BF16)
            wu_bf[...] = wu_ref[...].astype(BF16)
            wd_bf[...] = wd_ref[...].astype(BF16)

        @pl.when(jnp.logical_and(first, tile == 0))
        def _():
            experts(slot, lo, hi, False, gathers)

        @pl.when(jnp.logical_and(first, tile > 0))
        def _():
            experts(slot, lo, hi, False, both)

        @pl.when(jnp.logical_not(first))
        def _():
            experts(slot, lo, hi, True, [])

        @pl.when(s == n_steps - 1)
        def _():
            last_slot = (n_tiles - 1) % 2
            for r in range(tm):
                scatter_row(n_tiles - 1, last_slot, r)
            gather_wait(1 - last_slot)
            for sl in range(min(n_tiles, 2)):
                scatter_wait(sl)


def _moe(h, route_out, w_g, w_u, w_d, layer, plane_rows):
    n_tok, d = h.shape
    n_exp, _, ff = w_g.shape[1:]
    n_asg = 2 * n_tok
    tm = MOE_TILE_ROWS
    while n_asg % tm:
        tm //= 2
    assert tm >= 8 and ff % LANES == 0 and plane_rows >= n_tok
    n_tiles = n_asg // tm
    plan = _moe_plan(route_out, n_tok, n_exp, tm, plane_rows)
    w_in_map = lambda s, se, *_: (layer, se[s], 0, 0)
    grid_spec = pltpu.PrefetchScalarGridSpec(
        num_scalar_prefetch=len(plan),
        grid=(n_tiles + n_exp - 1,),
        in_specs=[pl.BlockSpec(memory_space=pl.ANY),
                  pl.BlockSpec((None, None, d, ff), w_in_map),
                  pl.BlockSpec((None, None, d, ff), w_in_map),
                  pl.BlockSpec((None, None, ff, d), w_in_map)],
        out_specs=pl.BlockSpec(memory_space=pl.ANY),
        scratch_shapes=[pltpu.VMEM((2, tm, d), F32),
                        pltpu.VMEM((2, tm, d), F32),
                        pltpu.VMEM((d, ff), BF16),
                        pltpu.VMEM((d, ff), BF16),
                        pltpu.VMEM((ff, d), BF16),
                        pltpu.SemaphoreType.DMA((2,)),
                        pltpu.SemaphoreType.DMA((2,))],
    )
    y = pl.pallas_call(
        functools.partial(_moe_kernel, n_tiles=n_tiles),
        out_shape=jax.ShapeDtypeStruct((2 * plane_rows, d), F32),
        grid_spec=grid_spec,
        name="moe_experts",
        compiler_params=pltpu.CompilerParams(dimension_semantics=("arbitrary",),
                                             vmem_limit_bytes=VMEM_LIMIT_BYTES,
                                             has_side_effects=True),
    )(*plan, h, w_g, w_u, w_d)
    return y.reshape(2, plane_rows, d)


def _swa_prompt_bias(w):
    r = jnp.arange(w)[:, None]
    c = jnp.arange(w)[None, :]
    first = jnp.where(c > r, -jnp.inf, 0.0).astype(F32)
    return jnp.stack([first, jnp.zeros((w, w), F32)])


def _swa_sample_bias(sg, ts, wb, grp):
    rho = jnp.arange(sg * ts)
    seq_q, t = rho // ts, rho % ts
    cc = jnp.arange(sg * wb)
    ok_c = (cc[None, :] // wb == seq_q[:, None]) & (cc[None, :] % wb - wb > t[:, None] - WINDOW)
    cn = jnp.arange(sg * ts)
    ok_n = (cn[None, :] // ts == seq_q[:, None]) & (cn[None, :] % ts <= t[:, None])
    to_bias = lambda ok: jnp.tile(jnp.where(ok, 0.0, -jnp.inf).astype(F32), (grp, 1))
    return to_bias(ok_c), to_bias(ok_n)


def kernel(x_prompt, x_sample, cache_win_k, cache_win_v, state_ret, c_prompt, c_sample,
           w_ada, b_ada, g_mix, g_ffn, w_in, attn_sinks, w_branch_attn, w_branch_ret, w_out,
           w_route_group, w_route_expert, w_exp_gate, w_exp_up, w_exp_down, g_final):
    bp, seq_len, d = x_prompt.shape
    bs, ts, _ = x_sample.shape
    depth = w_in.shape[0]
    n_heads = attn_sinks.shape[1]
    _, _, wb, n_kv, hd = cache_win_k.shape
    ret_heads, ret_hd = state_ret.shape[2], state_ret.shape[3]
    n_groups = w_route_group.shape[2]
    n_exp = w_route_expert.shape[2]
    per_group = n_exp // n_groups
    qw, kvw, rw = n_heads * hd, n_kv * hd, ret_heads * ret_hd
    assert ret_hd == LANES and LANES % hd == 0 and wb == WINDOW
    assert seq_len % WINDOW == 0 and seq_len % RET_CHUNK == 0 and ts < RET_CHUNK
    n_p, n_s = bp * seq_len, bs * ts
    n_tot = n_p + n_s
    ret_col0 = qw + 2 * kvw
    gate_col0 = ret_col0 + 4 * rw
    proj_tn = 512

    bp_pad = -(-bp // 8) * 8
    r_pad = -(-(n_s + bp_pad) // BF16_SUBLANES) * BF16_SUBLANES
    c_rows = jnp.concatenate([jnp.repeat(c_sample, ts, axis=0), c_prompt,
                              jnp.zeros((r_pad - n_s - bp, d), F32)], axis=0)
    mod_s, mod_p = _ada(c_rows, n_s, bp_pad, w_ada, b_ada)

    pos_p = jnp.arange(seq_len, dtype=I32)
    pos_s = PAST_LEN + jnp.arange(ts, dtype=I32)
    sg = SAMPLE_SEQS_PER_STEP
    swa_tabs_p = _rope_tables(pos_p, hd // 4, hd, ROPE_THETA)
    swa_tabs_s = tuple(jnp.tile(t, (sg, 1)) for t in _rope_tables(pos_s, hd // 4, hd, ROPE_THETA))
    ret_tabs_p = _ret_rope_tables(pos_p)
    ret_tabs_s = tuple(jnp.tile(t, (sg, 1)) for t in _ret_rope_tables(pos_s))
    dec_p = _ret_decay_tables(ret_heads, RET_CHUNK, RET_CHUNK)
    dec_s = _ret_decay_tables(ret_heads, ts, sg * ts)
    sel_s = jnp.broadcast_to(
        (jnp.arange(sg * ts)[None, :] // ts == jnp.arange(sg)[:, None]).astype(F32)[:, :, None],
        (sg, sg * ts, LANES))
    bias_p = _swa_prompt_bias(WINDOW)
    bias_sc, bias_sn = _swa_sample_bias(sg, ts, wb, n_heads // n_kv)

    w_route = jnp.concatenate(
        [w_route_group, w_route_expert,
         jnp.zeros((depth, d, ROUTER_LANES - n_groups - n_exp), F32)], axis=2)
    cache_k = cache_win_k.reshape(depth, bs * wb, kvw)
    cache_v = cache_win_v.reshape(depth, bs * wb, kvw)

    xp = x_prompt.reshape(n_p, d)
    xs = x_sample.reshape(n_s, d)
    moe_out = route_out = None
    outs = {k: [] for k in ("pk", "pv", "ps", "sk", "sv", "ss")}
    for l in range(depth):
        adds = [] if moe_out is None else [(moe_out, 0, 2), (moe_out, 1, 3)]
        gate = None if moe_out is None else (mod_p, mod_s, l - 1, 5)
        res = _norm_call(xp, xs, seq_len, adds, gate, g_mix[l].reshape(1, d),
                         (mod_p, mod_s, l, 1, 0), emit_x=moe_out is not None, h_dtype=BF16,
                         route_w=route_out, name=f"norm_mix_{l}")
        if moe_out is not None:
            xp, xs, h = res
        else:
            (h,) = res
        proj = _matmul(h, w_in, l, proj_tn, "in_proj")

        a, pk, pv = _swa_prompt(proj, attn_sinks, l, bp, seq_len, n_tot, n_heads, n_kv, hd,
                                swa_tabs_p, bias_p)
        a, k_new, v_new = _swa_sample(proj, a, attn_sinks, cache_k, cache_v, l, n_p, bs, ts,
                                      n_heads, n_kv, hd, swa_tabs_s, bias_sc, bias_sn)
        rn, st_p = _ret_prompt(proj, bp, seq_len, n_tot, ret_col0, ret_heads, ret_tabs_p, dec_p)
        rn, st_s = _ret_sample(proj, rn, state_ret, l, n_p, bs, ts, ret_col0, ret_heads,
                               ret_tabs_s, dec_s, sel_s)
        merged = _merge(a, rn, proj, gate_col0, w_branch_attn, w_branch_ret, l, min(proj_tn, d))
        mix = _matmul(merged, w_out, l, min(proj_tn, d), "out_proj")

        outs["pk"].append(pk.reshape(bp, wb, n_kv, hd))
        outs["pv"].append(pv.reshape(bp, wb, n_kv, hd))
        outs["ps"].append(st_p)
        outs["sk"].append(jnp.concatenate([cache_win_k[l][:, ts:], k_new.reshape(bs, ts, n_kv, hd)], axis=1))
        outs["sv"].append(jnp.concatenate([cache_win_v[l][:, ts:], v_new.reshape(bs, ts, n_kv, hd)], axis=1))
        outs["ss"].append(st_s)

        xp, xs, h2, route_out = _norm_call(
            xp, xs, seq_len, [(mix, None, None)], (mod_p, mod_s, l, 2), g_ffn[l].reshape(1, d),
            (mod_p, mod_s, l, 4, 3), emit_x=True, h_dtype=F32,
            route=(w_route[l], n_groups, per_group), name=f"norm_ffn_route_{l}")
        moe_out = _moe(h2, route_out, w_exp_gate, w_exp_up, w_exp_down, l, n_tot)

    yp, ys = _norm_call(xp, xs, seq_len, [(moe_out, 0, 2), (moe_out, 1, 3)],
                        (mod_p, mod_s, depth - 1, 5), g_final.reshape(1, d), None,
                        emit_x=False, final=True, route_w=route_out, name="norm_final")
    stack = lambda k: jnp.stack(outs[k])
    return (yp.reshape(bp, seq_len, d), ys.reshape(bs, ts, d),
            stack("pk"), stack("pv"), stack("ps"), stack("sk"), stack("sv"), stack("ss"))
```

```python
import functools

import jax
import jax.numpy as jnp
from jax import lax
from jax.experimental import pallas as pl
from jax.experimental.pallas import tpu as pltpu

F32 = jnp.float32
BF16 = jnp.bfloat16
I32 = jnp.int32

PAST_LEN = 16384
WINDOW = 128
ROPE_THETA = 500000.0
RET_CHUNK = 128
RET_ROT_BASE = 10000.0
EPS = 1e-6

LANES = 128
BF16_SUBLANES = 16
VMEM_LIMIT_BYTES = 56 * 1024 * 1024
ELEM_TILE_ROWS = 512
MATMUL_MAX_ROWS = 2048
SAMPLE_SEQS_PER_STEP = 8
MOE_TILE_ROWS = 256
MOE_OUT_CHUNKS = 4
ROUTER_LANES = 128


def _cparams(n_axes):
    return pltpu.CompilerParams(dimension_semantics=("arbitrary",) * n_axes,
                                vmem_limit_bytes=VMEM_LIMIT_BYTES)


def _matmul_rows(n):
    for d in range(min(n, MATMUL_MAX_ROWS), 0, -1):
        if n % d == 0 and d % BF16_SUBLANES == 0:
            return d
    raise ValueError(f"no row tile for {n} rows")


def _silu(x):
    return x * jax.nn.sigmoid(x)


def _ada_kernel(c_ref, w_ref, b_ref, os_ref, op_ref):
    s = _silu(c_ref[...]).astype(BF16)
    mod = jnp.dot(s, w_ref[0].astype(BF16), preferred_element_type=F32) + b_ref[0]
    n_s, n_p = os_ref.shape[1], op_ref.shape[1]
    os_ref[0] = mod[:n_s]
    op_ref[0] = mod[n_s:n_s + n_p]


def _ada(c_rows, n_s, n_p, w_ada, b_ada):
    depth, d, n6 = w_ada.shape
    r = c_rows.shape[0]
    tn = min(1024, d)
    assert n6 % tn == 0 and n_s % 8 == 0 and n_p % 8 == 0 and r >= n_s + n_p
    return pl.pallas_call(
        _ada_kernel,
        out_shape=[jax.ShapeDtypeStruct((depth, n_s, n6), F32),
                   jax.ShapeDtypeStruct((depth, n_p, n6), F32)],
        grid=(depth, n6 // tn),
        in_specs=[pl.BlockSpec((r, d), lambda l, j: (0, 0)),
                  pl.BlockSpec((1, d, tn), lambda l, j: (l, 0, j)),
                  pl.BlockSpec((1, 1, tn), lambda l, j: (l, 0, j))],
        out_specs=[pl.BlockSpec((1, n_s, tn), lambda l, j: (l, 0, j)),
                   pl.BlockSpec((1, n_p, tn), lambda l, j: (l, 0, j))],
        name="adaln_mod",
        compiler_params=_cparams(2),
    )(c_rows, w_ada, b_ada.reshape(depth, 1, n6))


def _route(logits, n_groups, per_group):
    lane = lax.broadcasted_iota(I32, logits.shape, 1).astype(F32)
    big = float(ROUTER_LANES)
    ninf = -jnp.inf
    gl = jnp.where(lane < n_groups, logits, ninf)
    gmax = jnp.max(gl, axis=1, keepdims=True)
    gsel = jnp.min(jnp.where(gl == gmax, lane, big), axis=1, keepdims=True)
    gden = jnp.sum(jnp.exp(gl - gmax), axis=1, keepdims=True)
    gw = 1.0 / gden
    lo = n_groups + gsel * per_group
    el = jnp.where(jnp.logical_and(lane >= lo, lane < lo + per_group), logits, ninf)
    v1 = jnp.max(el, axis=1, keepdims=True)
    i1 = jnp.min(jnp.where(el == v1, lane, big), axis=1, keepdims=True)
    el2 = jnp.where(lane == i1, ninf, el)
    v2 = jnp.max(el2, axis=1, keepdims=True)
    i2 = jnp.min(jnp.where(el2 == v2, lane, big), axis=1, keepdims=True)
    t = jnp.exp(v2 - v1)
    w1 = gw / (1.0 + t)
    w2 = gw * t / (1.0 + t)
    out = jnp.where(lane == 0, i1 - n_groups,
                    jnp.where(lane == 1, i2 - n_groups,
                              jnp.where(lane == 2, w1, jnp.where(lane == 3, w2, 0.0))))
    return out


def _norm_kernel(*refs, add_lanes, has_mod, emit_x, h_dtype, route_dims, final, n_ptiles,
                 n_srows, tiles_per_seq):
    n_add = len(add_lanes)
    weighted = any(l is not None for l in add_lanes)
    it = iter(refs)
    xp_ref, xs_ref = next(it), next(it)
    add_refs = [next(it) for _ in range(n_add)]
    if weighted:
        rw_ref = next(it)
    if n_add:
        gp_ref, gs_ref = next(it), next(it)
    gain_ref = next(it)
    if has_mod:
        scp_ref, scs_ref, shp_ref, shs_ref = next(it), next(it), next(it), next(it)
    if route_dims:
        wr_ref = next(it)
    if emit_x:
        xpo_ref, xso_ref = next(it), next(it)
    if h_dtype is not None:
        h_ref = next(it)
    if route_dims:
        r_ref = next(it)
    if final:
        yp_ref, ys_ref = next(it), next(it)

    def compute(x, adds, rw, gate, sc, sh):
        if adds:
            scaled = [a if l is None else a * rw[:, l:l + 1] for a, l in zip(adds, add_lanes)]
            s = scaled[0]
            for a in scaled[1:]:
                s = s + a
            x = x + gate * s
        h = x * lax.rsqrt(jnp.mean(x * x, axis=-1, keepdims=True) + EPS) * gain_ref[...]
        if has_mod:
            h = h * (1.0 + sc) + sh
        return x, h

    def logits_of(h):
        return jnp.dot(h, wr_ref[...], preferred_element_type=F32,
                       precision=lax.Precision.HIGHEST)

    i = pl.program_id(0)

    @pl.when(i < n_ptiles)
    def _():
        seq_row = pl.ds(i // tiles_per_seq, 1)
        adds = [a[...].astype(F32) for a in add_refs]
        x, h = compute(xp_ref[...], adds, rw_ref[...] if weighted else None,
                       gp_ref[seq_row, :] if n_add else None,
                       scp_ref[seq_row, :] if has_mod else None,
                       shp_ref[seq_row, :] if has_mod else None)
        if emit_x:
            xpo_ref[...] = x
        if h_dtype is not None:
            h_ref[...] = h.astype(h_dtype)
        if route_dims:
            r_ref[...] = _route(logits_of(h), *route_dims)
        if final:
            yp_ref[...] = h

    @pl.when(i == n_ptiles)
    def _():
        adds = [a[0:n_srows].astype(F32) for a in add_refs]
        x, h = compute(xs_ref[...], adds, rw_ref[0:n_srows] if weighted else None,
                       gs_ref[...] if n_add else None,
                       scs_ref[...] if has_mod else None,
                       shs_ref[...] if has_mod else None)
        if emit_x:
            xso_ref[...] = x
        if h_dtype is not None:
            h_ref[0:n_srows] = h.astype(h_dtype)
        if route_dims:
            r_ref[0:n_srows] = _route(logits_of(h), *route_dims)
        if final:
            ys_ref[...] = h


def _norm_call(xp, xs, seq_len, adds, gate, gain, mod, *, emit_x, name, h_dtype=None,
               route=None, route_w=None, final=False):
    n_p, d = xp.shape
    n_s = xs.shape[0]
    te = min(ELEM_TILE_ROWS, seq_len)
    assert seq_len % te == 0 and n_s <= te and n_s % 8 == 0
    tiles_per_seq = seq_len // te
    n_ptiles = n_p // te
    n_tot = n_p + n_s
    mod_p_rows = (gate if gate is not None else mod)[0].shape[1]

    def pidx(i):
        return jnp.minimum(i, n_ptiles - 1)

    def modp_spec(layer, chunk):
        return pl.BlockSpec((None, mod_p_rows, d), lambda i: (layer, 0, chunk))

    def mods_spec(layer, chunk):
        return pl.BlockSpec((None, n_s, d), lambda i: (layer, 0, chunk))

    args = [xp, xs]
    specs = [pl.BlockSpec((te, d), lambda i: (pidx(i), 0)),
             pl.BlockSpec((n_s, d), lambda i: (0, 0))]
    for arr, plane, _ in adds:
        args.append(arr)
        if plane is None:
            specs.append(pl.BlockSpec((te, d), lambda i: (i, 0)))
        else:
            specs.append(pl.BlockSpec((None, te, d), lambda i, plane=plane: (plane, i, 0)))
    if route_w is not None:
        args.append(route_w)
        specs.append(pl.BlockSpec((te, ROUTER_LANES), lambda i: (i, 0)))
    if adds:
        mp, ms, layer, chunk = gate
        args += [mp, ms]
        specs += [modp_spec(layer, chunk), mods_spec(layer, chunk)]
    args.append(gain)
    specs.append(pl.BlockSpec((1, d), lambda i: (0, 0)))
    if mod is not None:
        mp, ms, layer, sc_chunk, sh_chunk = mod
        args += [mp, ms, mp, ms]
        specs += [modp_spec(layer, sc_chunk), mods_spec(layer, sc_chunk),
                  modp_spec(layer, sh_chunk), mods_spec(layer, sh_chunk)]
    route_dims = None
    if route is not None:
        wr, n_groups, per_group = route
        route_dims = (n_groups, per_group)
        args.append(wr)
        specs.append(pl.BlockSpec(wr.shape, lambda i: (0, 0)))

    out_shapes, out_specs = [], []
    if emit_x:
        out_shapes += [jax.ShapeDtypeStruct((n_p, d), F32), jax.ShapeDtypeStruct((n_s, d), F32)]
        out_specs += [pl.BlockSpec((te, d), lambda i: (pidx(i), 0)),
                      pl.BlockSpec((n_s, d), lambda i: (0, 0))]
    if h_dtype is not None:
        out_shapes.append(jax.ShapeDtypeStruct((n_tot, d), h_dtype))
        out_specs.append(pl.BlockSpec((te, d), lambda i: (i, 0)))
    if route is not None:
        out_shapes.append(jax.ShapeDtypeStruct((n_tot, ROUTER_LANES), F32))
        out_specs.append(pl.BlockSpec((te, ROUTER_LANES), lambda i: (i, 0)))
    if final:
        out_shapes += [jax.ShapeDtypeStruct((n_p, d), F32), jax.ShapeDtypeStruct((n_s, d), F32)]
        out_specs += [pl.BlockSpec((te, d), lambda i: (pidx(i), 0)),
                      pl.BlockSpec((n_s, d), lambda i: (0, 0))]

    kern = functools.partial(_norm_kernel, add_lanes=tuple(l for _, _, l in adds),
                             has_mod=mod is not None, emit_x=emit_x, h_dtype=h_dtype,
                             route_dims=route_dims, final=final, n_ptiles=n_ptiles, n_srows=n_s,
                             tiles_per_seq=tiles_per_seq)
    return pl.pallas_call(
        kern, out_shape=out_shapes, grid=(n_ptiles + 1,),
        in_specs=specs, out_specs=out_specs, name=name, compiler_params=_cparams(1),
    )(*args)


def _mm_kernel(x_ref, w_ref, o_ref):
    o_ref[...] = jnp.dot(x_ref[...], w_ref[...].astype(BF16),
                         preferred_element_type=F32).astype(o_ref.dtype)


def _matmul(x, w, layer, tn, name, out_dtype=BF16):
    m, k = x.shape
    n = w.shape[2]
    tm = _matmul_rows(m)
    assert n % tn == 0
    return pl.pallas_call(
        _mm_kernel,
        out_shape=jax.ShapeDtypeStruct((m, n), out_dtype),
        grid=(m // tm, n // tn),
        in_specs=[pl.BlockSpec((tm, k), lambda i, j: (i, 0)),
                  pl.BlockSpec((None, k, tn), lambda i, j: (layer, 0, j))],
        out_specs=pl.BlockSpec((tm, tn), lambda i, j: (i, j)),
        name=name,
        compiler_params=_cparams(2),
    )(x, w)


def _merge_kernel(a_ref, r_ref, g0_ref, g1_ref, wa_ref, wb_ref, o_ref):
    ya = jnp.dot(a_ref[...], wa_ref[...].astype(BF16), preferred_element_type=F32)
    yb = jnp.dot(r_ref[...], wb_ref[...].astype(BF16), preferred_element_type=F32)
    g0 = jax.nn.sigmoid(g0_ref[...].astype(F32))
    g1 = jax.nn.sigmoid(g1_ref[...].astype(F32))
    o_ref[...] = (g0 * ya + g1 * yb).astype(o_ref.dtype)


def _merge(a, rn, proj, gate_col0, w_a, w_b, layer, tn):
    m, ka = a.shape
    kb = rn.shape[1]
    d = w_a.shape[2]
    tm = _matmul_rows(m)
    assert d % tn == 0 and gate_col0 % tn == 0
    c0 = gate_col0 // tn
    c1 = (gate_col0 + d) // tn
    return pl.pallas_call(
        _merge_kernel,
        out_shape=jax.ShapeDtypeStruct((m, d), BF16),
        grid=(m // tm, d // tn),
        in_specs=[pl.BlockSpec((tm, ka), lambda i, j: (i, 0)),
                  pl.BlockSpec((tm, kb), lambda i, j: (i, 0)),
                  pl.BlockSpec((tm, tn), lambda i, j: (i, c0 + j)),
                  pl.BlockSpec((tm, tn), lambda i, j: (i, c1 + j)),
                  pl.BlockSpec((None, ka, tn), lambda i, j: (layer, 0, j)),
                  pl.BlockSpec((None, kb, tn), lambda i, j: (layer, 0, j))],
        out_specs=pl.BlockSpec((tm, tn), lambda i, j: (i, j)),
        name="branch_merge",
        compiler_params=_cparams(2),
    )(a, rn, proj, proj, w_a, w_b)


def _rope_tables(pos, rot_dim, head_dim, base):
    half = rot_dim // 2
    inv = 1.0 / (base ** (jnp.arange(half, dtype=F32) * (2.0 / rot_dim)))
    ang = pos.astype(F32)[:, None] * inv[None, :]
    cos, sin = jnp.cos(ang), jnp.sin(ang)
    t = pos.shape[0]
    rest = head_dim - rot_dim
    cos_h = jnp.concatenate([cos, cos, jnp.ones((t, rest), F32)], axis=1)
    sin_lo = jnp.concatenate([jnp.zeros((t, half), F32), sin, jnp.zeros((t, rest), F32)], axis=1)
    sin_hi = jnp.concatenate([-sin, jnp.zeros((t, half + rest), F32)], axis=1)
    reps = LANES // head_dim
    tile = lambda a: jnp.tile(a, (1, reps))
    return tile(cos_h), tile(sin_lo), tile(sin_hi)


def _rope_lanes(x, cos, sin_lo, sin_hi, half):
    return (x * cos + pltpu.roll(x, half, axis=1) * sin_lo
            + pltpu.roll(x, LANES - half, axis=1) * sin_hi)


def _rope_wide(x, cos, sin_lo, sin_hi, half):
    cols = [_rope_lanes(x[:, c:c + LANES], cos, sin_lo, sin_hi, half)
            for c in range(0, x.shape[1], LANES)]
    return cols[0] if len(cols) == 1 else jnp.concatenate(cols, axis=1)


def _dot_nt(a, b):
    return lax.dot_general(a, b, (((1,), (1,)), ((), ())), preferred_element_type=F32)


def _dot_tn(a, b):
    return lax.dot_general(a, b, (((0,), (0,)), ((), ())), preferred_element_type=F32)


def _sink_softmax_pv(score_parts, value_parts, sink_col):
    m = sink_col
    for s in score_parts:
        m = jnp.maximum(m, jnp.max(s, axis=1, keepdims=True))
    den = jnp.exp(sink_col - m)
    acc = None
    for s, v in zip(score_parts, value_parts):
        p = jnp.exp(s - m)
        den = den + jnp.sum(p, axis=1, keepdims=True)
        pv = jnp.dot(p.astype(BF16), v, preferred_element_type=F32)
        acc = pv if acc is None else acc + pv
    return acc * (1.0 / den)


def _swa_prompt_kernel(sink_ref, q_ref, kc_ref, kp_ref, vc_ref, vp_ref,
                       cc_ref, lc_ref, hc_ref, cp_ref, lp_ref, hp_ref, bias_ref, as_ref,
                       a_ref, nk_ref, nv_ref, *, n_blocks, blocks_per_seq, **cfg):
    s = pl.program_id(0)

    @pl.when(s < n_blocks)
    def _():
        _swa_prompt_block(sink_ref, q_ref, kc_ref, kp_ref, vc_ref, vp_ref,
                          cc_ref, lc_ref, hc_ref, cp_ref, lp_ref, hp_ref, bias_ref,
                          a_ref, nk_ref, nv_ref,
                          last_of_seq=lax.rem(s, blocks_per_seq) == blocks_per_seq - 1, **cfg)

    @pl.when(s == n_blocks)
    def _():
        a_ref[0:as_ref.shape[0]] = as_ref[...]


def _swa_prompt_block(sink_ref, q_ref, kc_ref, kp_ref, vc_ref, vp_ref,
                      cc_ref, lc_ref, hc_ref, cp_ref, lp_ref, hp_ref, bias_ref,
                      a_ref, nk_ref, nv_ref, *, last_of_seq, layer, n_heads, n_kv, hd, half):
    w = q_ref.shape[0]
    grp = n_heads // n_kv
    per_tile = LANES // hd
    tabs_c = (cc_ref[...], lc_ref[...], hc_ref[...])
    tabs_p = (cp_ref[...], lp_ref[...], hp_ref[...])
    q = _rope_wide(q_ref[...].astype(F32), *tabs_c, half) * (hd ** -0.5)
    kc_f = _rope_wide(kc_ref[...].astype(F32), *tabs_c, half)
    kp_f = _rope_wide(kp_ref[...].astype(F32), *tabs_p, half)
    vc_f, vp_f = vc_ref[...].astype(F32), vp_ref[...].astype(F32)
    lane = lax.broadcasted_iota(I32, (w, LANES), 1)
    low = lane < hd
    upper = lax.broadcasted_iota(I32, (w, w), 1) > lax.broadcasted_iota(I32, (w, w), 0)
    bias = bias_ref[...]

    def both_halves(x, g):
        t = (g * hd) // LANES
        slab = x[:, t * LANES:(t + 1) * LANES]
        other = pltpu.roll(slab, hd, axis=1)
        return jnp.where(low, slab, other) if (g % per_tile) == 0 else jnp.where(low, other, slab)

    slabs = []
    for g in range(n_kv):
        kk = jnp.concatenate([both_halves(kp_f, g), both_halves(kc_f, g)], axis=0).astype(BF16)
        vv = jnp.concatenate([both_halves(vp_f, g), both_halves(vc_f, g)], axis=0).astype(BF16)
        heads = [g * grp + j for j in range(grp)]
        qg = jnp.concatenate(
            [jnp.where(low if h % per_tile == 0 else jnp.logical_not(low),
                       q[:, (h // per_tile) * LANES:(h // per_tile + 1) * LANES], 0.0)
             for h in heads], axis=0).astype(BF16)
        s_all = _dot_nt(qg, kk)
        probs, inv_den = [], []
        for j, h in enumerate(heads):
            s = jnp.where(upper, s_all[j * w:(j + 1) * w, :w], s_all[j * w:(j + 1) * w, w:]) + bias
            sink = sink_ref[layer, h]
            m = jnp.maximum(jnp.max(s, axis=1, keepdims=True), sink)
            p = jnp.exp(s - m)
            inv_den.append(1.0 / (jnp.sum(p, axis=1, keepdims=True) + jnp.exp(sink - m)))
            probs.append(jnp.concatenate([jnp.where(upper, p, 0.0), jnp.where(upper, 0.0, p)],
                                         axis=1).astype(BF16))
        o = jnp.dot(jnp.concatenate(probs, axis=0), vv, preferred_element_type=F32)
        o = [o[j * w:(j + 1) * w] * inv_den[j] for j in range(grp)]
        for j in range(0, grp, per_tile):
            slabs.append(jnp.where(low, o[j], o[j + 1]))
    a_ref[...] = jnp.concatenate(slabs, axis=1).astype(a_ref.dtype)

    @pl.when(last_of_seq)
    def _():
        nk_ref[...] = kc_f
        nv_ref[...] = vc_f


def _swa_prompt(proj, a_sample, sinks, layer, bp, seq_len, n_heads, n_kv, hd, tabs, bias):
    w = WINDOW
    nb = seq_len // w
    total = bp * nb
    n_s = a_sample.shape[0]
    qw, kvw = n_heads * hd, n_kv * hd
    assert qw % kvw == 0 and 2 * hd == LANES and (n_heads // n_kv) % 2 == 0 and n_s <= w
    kcol, vcol = qw // kvw, qw // kvw + 1
    blk = lambda s: jnp.minimum(s, total - 1)
    pos = lambda s: lax.rem(blk(s), nb)
    cur = lambda s: blk(s)
    prev = lambda s: blk(s) - jnp.minimum(pos(s), 1)
    tab_cur = pl.BlockSpec((w, LANES), lambda s: (pos(s), 0))
    tab_prev = pl.BlockSpec((w, LANES), lambda s: (jnp.maximum(pos(s) - 1, 0), 0))
    kern = functools.partial(_swa_prompt_kernel, n_blocks=total, blocks_per_seq=nb, layer=layer,
                             n_heads=n_heads, n_kv=n_kv, hd=hd, half=hd // 8)
    return pl.pallas_call(
        kern,
        out_shape=[jax.ShapeDtypeStruct((total * w + n_s, qw), BF16),
                   jax.ShapeDtypeStruct((bp, w, kvw), F32),
                   jax.ShapeDtypeStruct((bp, w, kvw), F32)],
        grid=(total + 1,),
        in_specs=[pl.BlockSpec(memory_space=pltpu.SMEM),
                  pl.BlockSpec((w, qw), lambda s: (cur(s), 0)),
                  pl.BlockSpec((w, kvw), lambda s: (cur(s), kcol)),
                  pl.BlockSpec((w, kvw), lambda s: (prev(s), kcol)),
                  pl.BlockSpec((w, kvw), lambda s: (cur(s), vcol)),
                  pl.BlockSpec((w, kvw), lambda s: (prev(s), vcol)),
                  tab_cur, tab_cur, tab_cur, tab_prev, tab_prev, tab_prev,
                  pl.BlockSpec((None, w, w), lambda s: (jnp.minimum(pos(s), 1), 0, 0)),
                  pl.BlockSpec((n_s, qw), lambda s: (0, 0))],
        out_specs=[pl.BlockSpec((w, qw), lambda s: (s, 0)),
                   pl.BlockSpec((None, w, kvw), lambda s: (blk(s) // nb, 0, 0)),
                   pl.BlockSpec((None, w, kvw), lambda s: (blk(s) // nb, 0, 0))],
        name="swa_prompt",
        compiler_params=_cparams(1),
    )(sinks, proj, proj, proj, proj, proj, *tabs, *tabs, bias, a_sample)


def _swa_sample_kernel(sink_ref, q_ref, k_ref, v_ref, kb_ref, vb_ref,
                       c_ref, l_ref, h_ref, bias_c_ref, bias_n_ref,
                       a_ref, nk_ref, nv_ref, *, layer, n_heads, n_kv, hd, half):
    rows = q_ref.shape[0]
    grp = n_heads // n_kv
    tabs = (c_ref[...], l_ref[...], h_ref[...])
    q = (_rope_wide(q_ref[...].astype(F32), *tabs, half) * (hd ** -0.5)).astype(BF16)
    kn_f = _rope_wide(k_ref[...].astype(F32), *tabs, half)
    kn = kn_f.astype(BF16)
    vn = v_ref[...]
    kb = kb_ref[...].astype(BF16)
    vb = vb_ref[...].astype(BF16)
    bias_c = bias_c_ref[...]
    bias_n = bias_n_ref[...]
    outs = []
    for g in range(n_kv):
        ks = slice(g * hd, (g + 1) * hd)
        qg = jnp.concatenate([q[:, (g * grp + j) * hd:(g * grp + j + 1) * hd]
                              for j in range(grp)], axis=0)
        sink = jnp.concatenate([jnp.full((rows, 1), sink_ref[layer, g * grp + j], F32)
                                for j in range(grp)], axis=0)
        s_c = _dot_nt(qg, kb[:, ks]) + bias_c
        s_n = _dot_nt(qg, kn[:, ks]) + bias_n
        o = _sink_softmax_pv([s_c, s_n], [vb[:, ks], vn[:, ks]], sink)
        outs += [o[j * rows:(j + 1) * rows] for j in range(grp)]
    a_ref[...] = jnp.concatenate(outs, axis=1).astype(a_ref.dtype)
    nk_ref[...] = kn_f
    nv_ref[...] = vn.astype(F32)


def _swa_sample(proj, sinks, cache_k, cache_v, layer, row0, bs, ts, n_heads, n_kv, hd,
                tabs, bias_c, bias_n):
    sg = SAMPLE_SEQS_PER_STEP
    rows = sg * ts
    wb = cache_k.shape[1] // bs
    qw, kvw = n_heads * hd, n_kv * hd
    assert bs % sg == 0 and row0 % rows == 0 and rows % BF16_SUBLANES == 0
    r0 = row0 // rows
    kcol, vcol = qw // kvw, qw // kvw + 1
    full = lambda shape: pl.BlockSpec(shape, lambda s: (0,) * len(shape))
    kern = functools.partial(_swa_sample_kernel, layer=layer, n_heads=n_heads, n_kv=n_kv,
                             hd=hd, half=hd // 8)
    return pl.pallas_call(
        kern,
        out_shape=[jax.ShapeDtypeStruct((bs * ts, qw), BF16),
                   jax.ShapeDtypeStruct((bs * ts, kvw), F32),
                   jax.ShapeDtypeStruct((bs * ts, kvw), F32)],
        grid=(bs // sg,),
        in_specs=[pl.BlockSpec(memory_space=pltpu.SMEM),
                  pl.BlockSpec((rows, qw), lambda s: (r0 + s, 0)),
                  pl.BlockSpec((rows, kvw), lambda s: (r0 + s, kcol)),
                  pl.BlockSpec((rows, kvw), lambda s: (r0 + s, vcol)),
                  pl.BlockSpec((None, sg * wb, kvw), lambda s: (layer, s, 0)),
                  pl.BlockSpec((None, sg * wb, kvw), lambda s: (layer, s, 0)),
                  full((rows, LANES)), full((rows, LANES)), full((rows, LANES)),
                  full(bias_c.shape), full(bias_n.shape)],
        out_specs=[pl.BlockSpec((rows, qw), lambda s: (s, 0)),
                   pl.BlockSpec((rows, kvw), lambda s: (s, 0)),
                   pl.BlockSpec((rows, kvw), lambda s: (s, 0))],
        name="swa_sample",
        compiler_params=_cparams(1),
    )(sinks, proj, proj, proj, cache_k, cache_v, *tabs, bias_c, bias_n)


def _ret_head(q_raw, k_raw, cos, sin, scale):
    half = LANES // 2
    q = q_raw * cos + pltpu.roll(q_raw, half, axis=1) * sin
    k = (k_raw * cos + pltpu.roll(k_raw, half, axis=1) * sin) * scale
    return q, k


def _ret_finish(o, g_raw):
    n = o * lax.rsqrt(jnp.mean(o * o, axis=-1, keepdims=True) + EPS)
    return n * _silu(g_raw)


def _ret_prompt_kernel(*refs, heads, heads_per_block, scale, n_chunks, chunks_per_seq):
    nb = heads // heads_per_block
    q_refs, k_refs, v_refs, g_refs = (refs[i * nb:(i + 1) * nb] for i in range(4))
    (cos_ref, sin_ref, intra_ref, qd_ref, kd_ref, cd_ref, rs_ref,
     rn_ref, st_ref, s_ref) = refs[4 * nb:]
    s = pl.program_id(0)
    c = lax.rem(s, chunks_per_seq)

    @pl.when(jnp.logical_and(s < n_chunks, c == 0))
    def _():
        s_ref[...] = jnp.zeros_like(s_ref)

    @pl.when(s < n_chunks)
    def _():
        cos, sin = cos_ref[...], sin_ref[...]
        for hh in range(heads):
            blk = hh // heads_per_block
            sl = slice((hh % heads_per_block) * LANES, (hh % heads_per_block + 1) * LANES)
            q, k = _ret_head(q_refs[blk][:, sl].astype(F32), k_refs[blk][:, sl].astype(F32),
                             cos, sin, scale)
            v = v_refs[blk][:, sl]
            att = _dot_nt(q.astype(BF16), k.astype(BF16)) * intra_ref[hh]
            s0 = s_ref[hh]
            o = (jnp.dot(att.astype(BF16), v, preferred_element_type=F32)
                 + jnp.dot((q * qd_ref[hh]).astype(BF16), s0.astype(BF16),
                           preferred_element_type=F32))
            s_ref[hh] = s0 * cd_ref[hh] + _dot_tn((k * kd_ref[hh]).astype(BF16), v)
            rn_ref[:, hh * LANES:(hh + 1) * LANES] = _ret_finish(
                o, g_refs[blk][:, sl].astype(F32)).astype(rn_ref.dtype)

    @pl.when(jnp.logical_and(s < n_chunks, c == chunks_per_seq - 1))
    def _():
        st_ref[...] = s_ref[...]

    @pl.when(s == n_chunks)
    def _():
        rn_ref[0:rs_ref.shape[0]] = rs_ref[...]


def _ret_prompt(proj, rn_sample, bp, seq_len, col0, n_heads, tabs, dec):
    ch = RET_CHUNK
    nc = seq_len // ch
    total = bp * nc
    n_s = rn_sample.shape[0]
    hw = n_heads * LANES
    hg = 4 if n_heads % 4 == 0 else n_heads
    bw = hg * LANES
    nb = n_heads // hg
    assert col0 % bw == 0 and hw % bw == 0 and n_s <= ch
    blk = lambda s: jnp.minimum(s, total - 1)
    inspecs = [pl.BlockSpec((ch, bw), lambda s, cb=(col0 + k * hw) // bw + j: (blk(s), cb))
               for k in range(4) for j in range(nb)]
    tab = pl.BlockSpec((ch, LANES), lambda s: (lax.rem(blk(s), nc), 0))
    dspec = pl.BlockSpec((n_heads, ch, LANES), lambda s: (0, 0, 0))
    kern = functools.partial(_ret_prompt_kernel, heads=n_heads, heads_per_block=hg,
                             scale=LANES ** -0.5, n_chunks=total, chunks_per_seq=nc)
    return pl.pallas_call(
        kern,
        out_shape=[jax.ShapeDtypeStruct((total * ch + n_s, hw), BF16),
                   jax.ShapeDtypeStruct((bp, n_heads, LANES, LANES), F32)],
        grid=(total + 1,),
        in_specs=inspecs + [tab, tab, dspec, dspec, dspec, dspec,
                            pl.BlockSpec((n_s, hw), lambda s: (0, 0))],
        out_specs=[pl.BlockSpec((ch, hw), lambda s: (s, 0)),
                   pl.BlockSpec((None, n_heads, LANES, LANES), lambda s: (blk(s) // nc, 0, 0, 0))],
        scratch_shapes=[pltpu.VMEM((n_heads, LANES, LANES), F32)],
        name="ret_prompt",
        compiler_params=_cparams(1),
    )(*([proj] * (4 * nb)), *tabs, *dec, rn_sample)


def _ret_sample_kernel(q_ref, k_ref, v_ref, g_ref, s_in_ref, cos_ref, sin_ref,
                       intra_ref, qd_ref, kd_ref, cd_ref, sel_ref,
                       rn_ref, s_out_ref, *, heads, seqs, scale):
    cos, sin = cos_ref[...], sin_ref[...]
    for hh in range(heads):
        sl = slice(hh * LANES, (hh + 1) * LANES)
        q, k = _ret_head(q_ref[:, sl].astype(F32), k_ref[:, sl].astype(F32), cos, sin, scale)
        v = v_ref[:, sl]
        att = _dot_nt(q.astype(BF16), k.astype(BF16)) * intra_ref[hh]
        o = jnp.dot(att.astype(BF16), v, preferred_element_type=F32)
        qd = (q * qd_ref[hh]).astype(BF16)
        kd = k * kd_ref[hh]
        for s in range(seqs):
            sel = sel_ref[s]
            s0 = s_in_ref[s, hh]
            o = o + sel * jnp.dot(qd, s0.astype(BF16), preferred_element_type=F32)
            s_out_ref[s, hh] = s0 * cd_ref[hh] + _dot_tn((kd * sel).astype(BF16), v)
        rn_ref[:, sl] = _ret_finish(o, g_ref[:, sl].astype(F32)).astype(rn_ref.dtype)


def _ret_sample(proj, state, layer, row0, bs, ts, col0, n_heads, tabs, dec, sel):
    sg = SAMPLE_SEQS_PER_STEP
    rows = sg * ts
    hw = n_heads * LANES
    hg = 4 if n_heads % 4 == 0 else n_heads
    bw = hg * LANES
    nhalf = n_heads // hg
    assert bs % sg == 0 and row0 % rows == 0 and col0 % bw == 0
    r0 = row0 // rows
    cb = lambda k: (col0 + k * hw) // bw
    inspec = lambda k: pl.BlockSpec((rows, bw), lambda s, hf: (r0 + s, cb(k) + hf))
    tab = pl.BlockSpec((rows, LANES), lambda s, hf: (0, 0))
    dspec = lambda a: pl.BlockSpec((hg,) + a.shape[1:], lambda s, hf: (hf, 0, 0))
    st_spec = pl.BlockSpec((None, sg, hg, LANES, LANES), lambda s, hf: (layer, s, hf, 0, 0))
    kern = functools.partial(_ret_sample_kernel, heads=hg, seqs=sg, scale=LANES ** -0.5)
    return pl.pallas_call(
        kern,
        out_shape=[jax.ShapeDtypeStruct((bs * ts, hw), BF16),
                   jax.ShapeDtypeStruct(state.shape[1:], F32)],
        grid=(bs // sg, nhalf),
        in_specs=[inspec(0), inspec(1), inspec(2), inspec(3), st_spec, tab, tab,
                  dspec(dec[0]), dspec(dec[1]), dspec(dec[2]), dspec(dec[3]),
                  pl.BlockSpec(sel.shape, lambda s, hf: (0, 0, 0))],
        out_specs=[pl.BlockSpec((rows, bw), lambda s, hf: (s, hf)),
                   pl.BlockSpec((sg, hg, LANES, LANES), lambda s, hf: (s, hf, 0, 0))],
        name="ret_sample",
        compiler_params=_cparams(2),
    )(proj, proj, proj, proj, state, *tabs, *dec, sel)


def _ret_decay_tables(n_heads, chunk, rows):
    log_g = jnp.log(1.0 - jnp.exp2(-5.0 - jnp.arange(n_heads, dtype=F32)))
    i = jnp.arange(chunk, dtype=F32)
    diff = i[:, None] - i[None, :]
    intra = jnp.where(diff >= 0, jnp.exp(jnp.maximum(diff, 0.0) * log_g[:, None, None]), 0.0)
    q_dec = jnp.exp((i + 1.0) * log_g[:, None])
    k_dec = jnp.exp((chunk - 1.0 - i) * log_g[:, None])
    c_dec = jnp.exp(chunk * log_g)
    reps = rows // chunk
    same = jnp.kron(jnp.eye(reps, dtype=F32), jnp.ones((chunk, chunk), F32))
    intra_t = jnp.tile(intra, (1, reps, reps)) * same[None]
    qd_t = jnp.broadcast_to(jnp.tile(q_dec, (1, reps))[:, :, None], (n_heads, rows, LANES))
    kd_t = jnp.broadcast_to(jnp.tile(k_dec, (1, reps))[:, :, None], (n_heads, rows, LANES))
    cd_t = jnp.broadcast_to(c_dec[:, None, None], (n_heads, LANES, LANES))
    return intra_t, qd_t, kd_t, cd_t


def _ret_rope_tables(pos):
    half = LANES // 2
    inv = 1.0 / (RET_ROT_BASE ** (jnp.arange(half, dtype=F32) * (2.0 / LANES)))
    ang = pos.astype(F32)[:, None] * inv[None, :]
    cos, sin = jnp.cos(ang), jnp.sin(ang)
    return jnp.concatenate([cos, cos], axis=1), jnp.concatenate([-sin, sin], axis=1)


def _moe_plan(route_out, n_tok, n_exp, tm, plane_rows):
    n_asg = 2 * n_tok
    n_tiles = n_asg // tm
    e_flat = route_out[:, 0:2].astype(I32).reshape(n_asg)
    order = jnp.argsort(e_flat, stable=True).astype(I32)
    tok = order >> 1
    dst = (order & 1) * plane_rows + tok
    eid = jnp.arange(n_exp, dtype=I32)
    shift = tm.bit_length() - 1
    tri = (eid[None, :] <= eid[:, None]).astype(I32)
    cnt = jnp.sum((e_flat[:, None] == eid[None, :]).astype(I32), axis=0)
    cend = jnp.sum(tri * cnt[None, :], axis=1)
    cstart = cend - cnt
    first_tile = cstart >> shift
    n_e = jnp.where(cnt > 0, ((cend - 1) >> shift) - first_tile + 1, 0)
    pend = jnp.sum(tri * n_e[None, :], axis=1)
    pstart = pend - n_e
    n_steps = jnp.sum(n_e)
    s = jnp.minimum(jnp.arange(n_tiles + n_exp, dtype=I32), n_steps - 1)
    se = jnp.sum((pend[None, :] <= s[:, None]).astype(I32), axis=1)
    onehot = (se[:, None] == eid[None, :]).astype(I32)
    pick = lambda table: jnp.sum(onehot * table[None, :], axis=1)
    st = pick(first_tile) + s - pick(pstart)
    lo = jnp.clip(pick(cstart) - st * tm, 0, tm)
    hi = jnp.clip(pick(cend) - st * tm, 0, tm)
    as_i32 = lambda a: a.astype(I32)
    return (as_i32(se), as_i32(st), as_i32(lo), as_i32(hi), as_i32(n_steps).reshape(1),
            as_i32(tok), as_i32(dst))


def _moe_kernel(se_ref, st_ref, lo_ref, hi_ref, ns_ref, tok_ref, dst_ref,
                h_hbm, wg_ref, wu_ref, wd_ref,
                y_hbm,
                gbuf, obuf, wg_bf, wu_bf, wd_bf, gsem, ssem, *, n_tiles):
    s = pl.program_id(0)
    n_steps = ns_ref[0]
    tm, d = obuf.shape[1], obuf.shape[2]
    ff = wg_bf.shape[1]

    def gather_row(tile, slot, r):
        pltpu.make_async_copy(h_hbm.at[pl.ds(tok_ref[tile * tm + r], 1)],
                              gbuf.at[slot, pl.ds(r, 1)], gsem.at[slot]).start()

    def gather_wait(slot):
        pltpu.make_async_copy(h_hbm.at[pl.ds(0, tm)], gbuf.at[slot], gsem.at[slot]).wait()

    def scatter_row(tile, slot, r):
        pltpu.make_async_copy(obuf.at[slot, pl.ds(r, 1)],
                              y_hbm.at[pl.ds(dst_ref[tile * tm + r], 1)], ssem.at[slot]).start()

    def scatter_wait(slot):
        pltpu.make_async_copy(obuf.at[slot], y_hbm.at[pl.ds(0, tm)], ssem.at[slot]).wait()

    def experts(slot, lo, hi, accumulate, row_dmas):
        n_chunks = MOE_OUT_CHUNKS
        cw = d // n_chunks
        n_gaps = 3 + n_chunks
        per_gap = -(-len(row_dmas) // n_gaps)

        def issue(k):
            for thunk in row_dmas[k * per_gap:(k + 1) * per_gap]:
                thunk()

        x = gbuf[slot].astype(BF16)
        issue(0)
        g = jnp.dot(x, wg_bf[...], preferred_element_type=F32)
        issue(1)
        u = jnp.dot(x, wu_bf[...], preferred_element_type=F32)
        issue(2)
        row = lax.broadcasted_iota(I32, (tm, ff), 0)
        owned = jnp.logical_and(row >= lo, row < hi)
        hid = jnp.where(owned, _silu(g) * u, 0.0).astype(BF16)
        for c in range(n_chunks):
            cols = slice(c * cw, (c + 1) * cw)
            y = jnp.dot(hid, wd_bf[:, cols], preferred_element_type=F32)
            if accumulate:
                obuf[slot, :, cols] += y
            else:
                obuf[slot, :, cols] = y
            issue(3 + c)

    @pl.when(s < n_steps)
    def _():
        tile = st_ref[s]
        slot = lax.rem(tile, 2)
        prev = jnp.maximum(s - 1, 0)
        first = jnp.logical_or(s == 0, tile != st_ref[prev])
        expert_changed = jnp.logical_or(s == 0, se_ref[s] != se_ref[prev])
        lo, hi = lo_ref[s], hi_ref[s]
        nxt = jnp.minimum(tile + 1, n_tiles - 1)

        @pl.when(s == 0)
        def _():
            for r in range(tm):
                gather_row(0, 0, r)

        @pl.when(first)
        def _():
            gather_wait(slot)

        @pl.when(jnp.logical_and(first, tile >= 2))
        def _():
            scatter_wait(slot)

        @pl.when(expert_changed)
        def _():
            wg_bf[...] = wg_ref[...].astype(BF16)
            wu_bf[...] = wu_ref[...].astype(BF16)
            wd_bf[...] = wd_ref[...].astype(BF16)

        for par in range(2):
            gathers = [functools.partial(gather_row, nxt, 1 - par, r) for r in range(tm)]
            scatters = [functools.partial(scatter_row, tile - 1, 1 - par, r) for r in range(tm)]
            both = [t for pair in zip(gathers, scatters) for t in pair]
            mine = jnp.logical_and(first, slot == par)

            if par == 0:
                @pl.when(jnp.logical_and(mine, tile == 0))
                def _():
                    experts(0, lo, hi, False, gathers)

            @pl.when(jnp.logical_and(mine, tile > 0))
            def _():
                experts(par, lo, hi, False, both)

        @pl.when(jnp.logical_not(first))
        def _():
            experts(slot, lo, hi, True, [])

        @pl.when(s == n_steps - 1)
        def _():
            last_slot = (n_tiles - 1) % 2
            for r in range(tm):
                scatter_row(n_tiles - 1, last_slot, r)
            gather_wait(1 - last_slot)
            for sl in range(min(n_tiles, 2)):
                scatter_wait(sl)


def _moe(h, route_out, w_g, w_u, w_d, layer, plane_rows):
    n_tok, d = h.shape
    n_exp, _, ff = w_g.shape[1:]
    n_asg = 2 * n_tok
    tm = MOE_TILE_ROWS
    while n_asg % tm:
        tm //= 2
    assert tm >= 8 and ff % LANES == 0 and plane_rows >= n_tok
    n_tiles = n_asg // tm
    plan = _moe_plan(route_out, n_tok, n_exp, tm, plane_rows)
    w_in_map = lambda s, se, *_: (layer, se[s], 0, 0)
    grid_spec = pltpu.PrefetchScalarGridSpec(
        num_scalar_prefetch=len(plan),
        grid=(n_tiles + n_exp - 1,),
        in_specs=[pl.BlockSpec(memory_space=pl.ANY),
                  pl.BlockSpec((None, None, d, ff), w_in_map),
                  pl.BlockSpec((None, None, d, ff), w_in_map),
                  pl.BlockSpec((None, None, ff, d), w_in_map)],
        out_specs=pl.BlockSpec(memory_space=pl.ANY),
        scratch_shapes=[pltpu.VMEM((2, tm, d), F32),
                        pltpu.VMEM((2, tm, d), F32),
                        pltpu.VMEM((d, ff), BF16),
                        pltpu.VMEM((d, ff), BF16),
                        pltpu.VMEM((ff, d), BF16),
                        pltpu.SemaphoreType.DMA((2,)),
                        pltpu.SemaphoreType.DMA((2,))],
    )
    y = pl.pallas_call(
        functools.partial(_moe_kernel, n_tiles=n_tiles),
        out_shape=jax.ShapeDtypeStruct((2 * plane_rows, d), F32),
        grid_spec=grid_spec,
        name="moe_experts",
        compiler_params=pltpu.CompilerParams(dimension_semantics=("arbitrary",),
                                             vmem_limit_bytes=VMEM_LIMIT_BYTES,
                                             has_side_effects=True),
    )(*plan, h, w_g, w_u, w_d)
    return y.reshape(2, plane_rows, d)


def _swa_prompt_bias(w):
    r = jnp.arange(w)[:, None]
    c = jnp.arange(w)[None, :]
    first = jnp.where(c > r, -jnp.inf, 0.0).astype(F32)
    return jnp.stack([first, jnp.zeros((w, w), F32)])


def _swa_sample_bias(sg, ts, wb, grp):
    rho = jnp.arange(sg * ts)
    seq_q, t = rho // ts, rho % ts
    cc = jnp.arange(sg * wb)
    ok_c = (cc[None, :] // wb == seq_q[:, None]) & (cc[None, :] % wb - wb > t[:, None] - WINDOW)
    cn = jnp.arange(sg * ts)
    ok_n = (cn[None, :] // ts == seq_q[:, None]) & (cn[None, :] % ts <= t[:, None])
    to_bias = lambda ok: jnp.tile(jnp.where(ok, 0.0, -jnp.inf).astype(F32), (grp, 1))
    return to_bias(ok_c), to_bias(ok_n)


def kernel(x_prompt, x_sample, cache_win_k, cache_win_v, state_ret, c_prompt, c_sample,
           w_ada, b_ada, g_mix, g_ffn, w_in, attn_sinks, w_branch_attn, w_branch_ret, w_out,
           w_route_group, w_route_expert, w_exp_gate, w_exp_up, w_exp_down, g_final):
    bp, seq_len, d = x_prompt.shape
    bs, ts, _ = x_sample.shape
    depth = w_in.shape[0]
    n_heads = attn_sinks.shape[1]
    _, _, wb, n_kv, hd = cache_win_k.shape
    ret_heads, ret_hd = state_ret.shape[2], state_ret.shape[3]
    n_groups = w_route_group.shape[2]
    n_exp = w_route_expert.shape[2]
    per_group = n_exp // n_groups
    qw, kvw, rw = n_heads * hd, n_kv * hd, ret_heads * ret_hd
    assert ret_hd == LANES and LANES % hd == 0 and wb == WINDOW
    assert seq_len % WINDOW == 0 and seq_len % RET_CHUNK == 0 and ts < RET_CHUNK
    n_p, n_s = bp * seq_len, bs * ts
    n_tot = n_p + n_s
    ret_col0 = qw + 2 * kvw
    gate_col0 = ret_col0 + 4 * rw
    proj_tn = 512

    bp_pad = -(-bp // 8) * 8
    r_pad = -(-(n_s + bp_pad) // BF16_SUBLANES) * BF16_SUBLANES
    c_rows = jnp.concatenate([jnp.repeat(c_sample, ts, axis=0), c_prompt,
                              jnp.zeros((r_pad - n_s - bp, d), F32)], axis=0)
    mod_s, mod_p = _ada(c_rows, n_s, bp_pad, w_ada, b_ada)

    pos_p = jnp.arange(seq_len, dtype=I32)
    pos_s = PAST_LEN + jnp.arange(ts, dtype=I32)
    sg = SAMPLE_SEQS_PER_STEP
    swa_tabs_p = _rope_tables(pos_p, hd // 4, hd, ROPE_THETA)
    swa_tabs_s = tuple(jnp.tile(t, (sg, 1)) for t in _rope_tables(pos_s, hd // 4, hd, ROPE_THETA))
    ret_tabs_p = _ret_rope_tables(pos_p)
    ret_tabs_s = tuple(jnp.tile(t, (sg, 1)) for t in _ret_rope_tables(pos_s))
    dec_p = _ret_decay_tables(ret_heads, RET_CHUNK, RET_CHUNK)
    dec_s = _ret_decay_tables(ret_heads, ts, sg * ts)
    sel_s = jnp.broadcast_to(
        (jnp.arange(sg * ts)[None, :] // ts == jnp.arange(sg)[:, None]).astype(F32)[:, :, None],
        (sg, sg * ts, LANES))
    bias_p = _swa_prompt_bias(WINDOW)
    bias_sc, bias_sn = _swa_sample_bias(sg, ts, wb, n_heads // n_kv)

    w_route = jnp.concatenate(
        [w_route_group, w_route_expert,
         jnp.zeros((depth, d, ROUTER_LANES - n_groups - n_exp), F32)], axis=2)
    cache_k = cache_win_k.reshape(depth, bs * wb, kvw)
    cache_v = cache_win_v.reshape(depth, bs * wb, kvw)

    xp = x_prompt.reshape(n_p, d)
    xs = x_sample.reshape(n_s, d)
    moe_out = route_out = None
    outs = {k: [] for k in ("pk", "pv", "ps", "sk", "sv", "ss")}
    for l in range(depth):
        adds = [] if moe_out is None else [(moe_out, 0, 2), (moe_out, 1, 3)]
        gate = None if moe_out is None else (mod_p, mod_s, l - 1, 5)
        res = _norm_call(xp, xs, seq_len, adds, gate, g_mix[l].reshape(1, d),
                         (mod_p, mod_s, l, 1, 0), emit_x=moe_out is not None, h_dtype=BF16,
                         route_w=route_out, name=f"norm_mix_{l}")
        if moe_out is not None:
            xp, xs, h = res
        else:
            (h,) = res
        proj = _matmul(h, w_in, l, proj_tn, "in_proj")

        a_s, k_new, v_new = _swa_sample(proj, attn_sinks, cache_k, cache_v, l, n_p, bs, ts,
                                        n_heads, n_kv, hd, swa_tabs_s, bias_sc, bias_sn)
        a, pk, pv = _swa_prompt(proj, a_s, attn_sinks, l, bp, seq_len, n_heads, n_kv, hd,
                                swa_tabs_p, bias_p)
        rn_s, st_s = _ret_sample(proj, state_ret, l, n_p, bs, ts, ret_col0, ret_heads,
                                 ret_tabs_s, dec_s, sel_s)
        rn, st_p = _ret_prompt(proj, rn_s, bp, seq_len, ret_col0, ret_heads, ret_tabs_p, dec_p)
        merged = _merge(a, rn, proj, gate_col0, w_branch_attn, w_branch_ret, l, min(proj_tn, d))
        mix = _matmul(merged, w_out, l, min(proj_tn, d), "out_proj")

        outs["pk"].append(pk.reshape(bp, wb, n_kv, hd))
        outs["pv"].append(pv.reshape(bp, wb, n_kv, hd))
        outs["ps"].append(st_p)
        outs["sk"].append(jnp.concatenate([cache_win_k[l][:, ts:], k_new.reshape(bs, ts, n_kv, hd)], axis=1))
        outs["sv"].append(jnp.concatenate([cache_win_v[l][:, ts:], v_new.reshape(bs, ts, n_kv, hd)], axis=1))
        outs["ss"].append(st_s)

        xp, xs, h2, route_out = _norm_call(
            xp, xs, seq_len, [(mix, None, None)], (mod_p, mod_s, l, 2), g_ffn[l].reshape(1, d),
            (mod_p, mod_s, l, 4, 3), emit_x=True, h_dtype=F32,
            route=(w_route[l], n_groups, per_group), name=f"norm_ffn_route_{l}")
        moe_out = _moe(h2, route_out, w_exp_gate, w_exp_up, w_exp_down, l, n_tot)

    yp, ys = _norm_call(xp, xs, seq_len, [(moe_out, 0, 2), (moe_out, 1, 3)],
                        (mod_p, mod_s, depth - 1, 5), g_final.reshape(1, d), None,
                        emit_x=False, final=True, route_w=route_out, name="norm_final")
    stack = lambda k: jnp.stack(outs[k])
    return (yp.reshape(bp, seq_len, d), ys.reshape(bs, ts, d),
            stack("pk"), stack("pv"), stack("ps"), stack("sk"), stack("sv"), stack("ss"))
```
